```python
import jax, jax.numpy as jnp
from jax import lax
import numpy as np

D_MODEL = 1024
BATCH = 8
SEQ = 4096
DEPTH = 1

D_MIX = D_MODEL
MLSTM_HEADS = 4
D_MLSTM = D_MIX // 2
MLSTM_HEAD_DIM = D_MLSTM // MLSTM_HEADS
MLSTM_CHUNK = 64
QK_CONV_WIDTH = 4
POOL_WINDOWS = (2, 4, 8, 16)
POOL_GROUPS = len(POOL_WINDOWS)
D_POOL = D_MIX - D_MLSTM
POOL_GROUP_DIM = D_POOL // POOL_GROUPS
D_FF = ((8 * D_MODEL // 3 + 127) // 128) * 128
FFN_CONV_WIDTH = 3
N_MOD = 6
EPS = 1e-6
D_IN_PROJ = 4 * D_MLSTM + D_POOL + 2 * MLSTM_HEADS

kernel_name = 'hybrid_mlstm_pool_convffn_adaln'


def rms_norm(x, g):
    xf = x.astype(jnp.float32)
    y = xf * lax.rsqrt(jnp.mean(xf * xf, axis=-1, keepdims=True) + EPS)
    return (y * g.astype(jnp.float32)).astype(x.dtype)


def causal_depthwise_conv(x, w, b):
    k, ch = w.shape
    y = lax.conv_general_dilated(
        x, w[:, None, :].astype(x.dtype), window_strides=(1,), padding=[(k - 1, 0)],
        dimension_numbers=('NWC', 'WIO', 'NWC'), feature_group_count=ch)
    return y + b.astype(x.dtype)


def mlstm_chunkwise(q, k, v, i_pre, log_f):
    bsz, nh, s, dh = q.shape
    L = MLSTM_CHUNK
    nc = s // L

    def to_chunks(a):
        a = a.reshape(bsz, nh, nc, L, *a.shape[3:])
        return jnp.moveaxis(a, 2, 0)

    qc, kc, vc = to_chunks(q), to_chunks(k), to_chunks(v)
    ic = to_chunks(i_pre)
    bc = jnp.cumsum(to_chunks(log_f), axis=-1)
    causal = jnp.tril(jnp.ones((L, L), dtype=bool))

    def step(carry, inp):
        c_state, n_state, m_state = carry
        qt, kt, vt, it, bt = inp
        log_d = bt[..., :, None] - bt[..., None, :] + it[..., None, :]
        log_d = jnp.where(causal, log_d, -jnp.inf)
        m_inter = bt + m_state[..., None]
        m_t = jnp.maximum(m_inter, jnp.max(log_d, axis=-1))
        w_inter = jnp.exp(m_inter - m_t)
        scores = jnp.einsum('bhtd,bhsd->bhts', qt, kt) * jnp.exp(log_d - m_t[..., None])
        num = (jnp.einsum('bhts,bhsd->bhtd', scores, vt)
               + w_inter[..., None] * jnp.einsum('bhvk,bhtk->bhtv', c_state, qt))
        den = jnp.sum(scores, axis=-1) + w_inter * jnp.einsum('bhk,bhtk->bht', n_state, qt)
        h = num / jnp.maximum(jnp.abs(den), jnp.exp(-m_t))[..., None]
        m_new = m_t[..., -1]
        w_state = jnp.exp(bt[..., -1:] - bt + it - m_new[..., None])
        decay = jnp.exp(bt[..., -1] + m_state - m_new)
        c_new = decay[..., None, None] * c_state + jnp.einsum('bhs,bhsv,bhsk->bhvk', w_state, vt, kt)
        n_new = decay[..., None] * n_state + jnp.einsum('bhs,bhsk->bhk', w_state, kt)
        return (c_new, n_new, m_new), h

    init = (jnp.zeros((bsz, nh, dh, dh), jnp.float32),
            jnp.zeros((bsz, nh, dh), jnp.float32),
            jnp.zeros((bsz, nh), jnp.float32))
    _, hs = lax.scan(step, init, (qc, kc, vc, ic, bc))
    return jnp.transpose(hs, (1, 0, 3, 2, 4)).reshape(bsz, s, nh * dh)


def causal_multiscale_pool(u):
    bsz, s, _ = u.shape
    uf = u.astype(jnp.float32).reshape(bsz, s, POOL_GROUPS, POOL_GROUP_DIM)
    cs = jnp.concatenate([jnp.zeros((bsz, 1, POOL_GROUPS, POOL_GROUP_DIM), jnp.float32),
                          jnp.cumsum(uf, axis=1)], axis=1)
    t = jnp.arange(s)
    win = jnp.array(POOL_WINDOWS, dtype=jnp.int32)
    lo = jnp.maximum(t[:, None] + 1 - win[None, :], 0)
    g_idx = jnp.arange(POOL_GROUPS)[None, :]
    lo_sum = cs[:, lo, g_idx]
    count = (t[:, None] + 1 - lo).astype(jnp.float32)
    mean = (cs[:, 1:] - lo_sum) / count[None, :, :, None]
    return mean - uf


def setup_inputs(seed: int = 0) -> dict:
    key = jax.random.key(seed)
    ks = jax.random.split(key, 24)

    def nrm(k, shape, scale):
        return jax.random.normal(k, shape, jnp.float32) * scale

    return {
        'x': nrm(ks[0], (BATCH, SEQ, D_MODEL), 1.0),
        'c': nrm(ks[1], (BATCH, D_MODEL), 1.0),
        'w_ada': nrm(ks[2], (DEPTH, D_MODEL, N_MOD * D_MODEL), D_MODEL ** -0.5),
        'b_ada': nrm(ks[3], (DEPTH, N_MOD * D_MODEL), 0.02),
        'norm1_g': 1.0 + nrm(ks[4], (DEPTH, D_MODEL), 0.02),
        'w_in': nrm(ks[5], (DEPTH, D_MODEL, D_IN_PROJ), D_MODEL ** -0.5),
        'qk_conv_w': nrm(ks[6], (DEPTH, QK_CONV_WIDTH, 2 * D_MLSTM), QK_CONV_WIDTH ** -0.5),
        'qk_conv_b': nrm(ks[7], (DEPTH, 2 * D_MLSTM), 0.02),
        'b_igate': nrm(ks[8], (DEPTH, MLSTM_HEADS), 0.1),
        'b_fgate': jnp.linspace(3.0, 6.0, MLSTM_HEADS, dtype=jnp.float32)[None, :]
                   + nrm(ks[9], (DEPTH, MLSTM_HEADS), 0.1),
        'mlstm_norm_g': 1.0 + nrm(ks[10], (DEPTH, D_MLSTM), 0.02),
        'pool_w': nrm(ks[11], (DEPTH, POOL_GROUPS, POOL_GROUP_DIM, POOL_GROUP_DIM), POOL_GROUP_DIM ** -0.5),
        'pool_scale': 1.0 + nrm(ks[12], (DEPTH, D_POOL), 0.02),
        'w_out': nrm(ks[13], (DEPTH, D_MIX, D_MODEL), D_MIX ** -0.5),
        'norm2_g': 1.0 + nrm(ks[14], (DEPTH, D_MODEL), 0.02),
        'w_up': nrm(ks[15], (DEPTH, D_MODEL, 2 * D_FF), D_MODEL ** -0.5),
        'ffn_conv_w': nrm(ks[16], (DEPTH, FFN_CONV_WIDTH, 2 * D_FF), FFN_CONV_WIDTH ** -0.5),
        'ffn_conv_b': nrm(ks[17], (DEPTH, 2 * D_FF), 0.02),
        'w_down': nrm(ks[18], (DEPTH, D_FF, D_MODEL), D_FF ** -0.5),
        'final_norm_g': 1.0 + nrm(ks[19], (D_MODEL,), 0.02),
    }


def reference(x, c, w_ada, b_ada, norm1_g, w_in, qk_conv_w, qk_conv_b, b_igate, b_fgate,
              mlstm_norm_g, pool_w, pool_scale, w_out, norm2_g, w_up, ffn_conv_w, ffn_conv_b,
              w_down, final_norm_g):
    bsz, s, _ = x.shape
    cond = jax.nn.silu(c)
    split_points = [D_MLSTM, 2 * D_MLSTM, 3 * D_MLSTM, 4 * D_MLSTM,
                    4 * D_MLSTM + D_POOL, 4 * D_MLSTM + D_POOL + MLSTM_HEADS]

    def heads(a):
        return a.astype(jnp.float32).reshape(bsz, s, MLSTM_HEADS, MLSTM_HEAD_DIM).transpose(0, 2, 1, 3)

    for l in range(DEPTH):
        mod = cond @ w_ada[l] + b_ada[l]
        shift1, scale1, gate1, shift2, scale2, gate2 = [m[:, None, :] for m in jnp.split(mod, N_MOD, axis=-1)]

        hmix = rms_norm(x, norm1_g[l]) * (1 + scale1) + shift1
        proj = hmix @ w_in[l]
        q_pre, k_pre, v, o_pre, u, i_raw, f_raw = jnp.split(proj, split_points, axis=-1)

        qk = jax.nn.silu(causal_depthwise_conv(jnp.concatenate([q_pre, k_pre], axis=-1),
                                               qk_conv_w[l], qk_conv_b[l]))
        q, k = jnp.split(qk, 2, axis=-1)
        i_pre = (i_raw + b_igate[l]).astype(jnp.float32).transpose(0, 2, 1)
        log_f = jax.nn.log_sigmoid((f_raw + b_fgate[l]).astype(jnp.float32)).transpose(0, 2, 1)
        h_m = mlstm_chunkwise(heads(q), heads(k) * (MLSTM_HEAD_DIM ** -0.5), heads(v), i_pre, log_f)
        h_m = rms_norm(h_m.astype(x.dtype), mlstm_norm_g[l]) * jax.nn.sigmoid(o_pre)

        pooled = causal_multiscale_pool(u)
        h_p = (jnp.einsum('bsgc,gcd->bsgd', pooled, pool_w[l].astype(jnp.float32))
               .reshape(bsz, s, D_POOL) * pool_scale[l]).astype(x.dtype)

        mixed = jnp.concatenate([h_m, h_p], axis=-1) @ w_out[l]
        x = x + gate1 * mixed

        hff = rms_norm(x, norm2_g[l]) * (1 + scale2) + shift2
        up = causal_depthwise_conv(hff @ w_up[l], ffn_conv_w[l], ffn_conv_b[l])
        g, val = jnp.split(up, 2, axis=-1)
        x = x + gate2 * ((jax.nn.silu(g) * val) @ w_down[l])

    return rms_norm(x, final_norm_g)
```

```python
import functools

import jax
import jax.numpy as jnp
from jax import lax
from jax.experimental import pallas as pl
from jax.experimental.pallas import tpu as pltpu

D_MODEL = 1024
MLSTM_HEADS = 4
D_MLSTM = 512
HEAD_DIM = 128
QK_CONV_WIDTH = 4
POOL_WINDOWS = (2, 4, 8, 16)
D_POOL = 512
POOL_GROUP_DIM = 128
D_FF = 2816
FFN_CONV_WIDTH = 3
N_MOD = 6
EPS = 1e-6

SUBLANES = 8
LANES = 128
VMEM_LIMIT_BYTES = 56 * 1024 * 1024

TIME_TILE = 512
MLSTM_CHUNK = 64
FF_BLOCK = 256
POOL_TAIL = 16
CONV_TAIL = SUBLANES

F32 = jnp.float32
BF16 = jnp.bfloat16


def _rms(x):
    return x * lax.rsqrt(jnp.mean(x * x, axis=-1, keepdims=True) + EPS)


def _log_sigmoid(x):
    return jnp.minimum(x, 0.0) - jnp.log1p(jnp.exp(-jnp.abs(x)))


def _ada_kernel(c_ref, w_ref, b_ref, o_ref):
    cond = c_ref[...]
    cond = cond * jax.nn.sigmoid(cond)
    o_ref[...] = jnp.dot(cond, w_ref[...], precision=lax.Precision.HIGHEST,
                         preferred_element_type=F32) + b_ref[...]


def _ada_mod(c, w_ada, b_ada):
    bsz = c.shape[0]
    n = w_ada.shape[1]
    blk = D_MODEL
    return pl.pallas_call(
        _ada_kernel,
        grid=(n // blk,),
        in_specs=[pl.BlockSpec((bsz, D_MODEL), lambda i: (0, 0)),
                  pl.BlockSpec((D_MODEL, blk), lambda i: (0, i)),
                  pl.BlockSpec((1, blk), lambda i: (0, i))],
        out_specs=pl.BlockSpec((bsz, blk), lambda i: (0, i)),
        out_shape=jax.ShapeDtypeStruct((bsz, n), F32),
        name="ada_mod",
    )(c, w_ada, b_ada.reshape(1, n))


def _mixer_kernel(x_ref, mod_ref, g1_ref, win_ref, wg_ref, wgt_ref, bgc_ref, bgr_ref,
                  cw_ref, cb_ref, gm_ref, pw_ref, ps_ref, wout_ref, o_ref,
                  qk_buf, u_buf, h_buf, cat_buf, c_state, n_state, m_state):
    T = TIME_TILE
    L = MLSTM_CHUNK
    j = pl.program_id(1)

    @pl.when(j == 0)
    def _():
        qk_buf[0:CONV_TAIL, :] = jnp.zeros((CONV_TAIL, 2 * D_MLSTM), F32)
        u_buf[0:POOL_TAIL, :] = jnp.zeros((POOL_TAIL, D_POOL), F32)
        c_state[...] = jnp.zeros_like(c_state)
        n_state[...] = jnp.zeros_like(n_state)
        m_state[...] = jnp.zeros_like(m_state)

    x = x_ref[0]
    shift1 = mod_ref[0, 0:1, :]
    scale1 = mod_ref[0, 1:2, :]
    gate1 = mod_ref[0, 2:3, :]
    hmix = (_rms(x) * g1_ref[...] * (1.0 + scale1) + shift1).astype(BF16)

    def proj(lo, hi):
        return jnp.dot(hmix, win_ref[:, lo:hi], preferred_element_type=F32)

    qk_buf[CONV_TAIL:CONV_TAIL + T, :] = proj(0, 2 * D_MLSTM)
    conv = cb_ref[...]
    for kk in range(QK_CONV_WIDTH):
        off = CONV_TAIL - (QK_CONV_WIDTH - 1) + kk
        conv = conv + cw_ref[kk:kk + 1, :] * qk_buf[pl.ds(off, T), :]
    qk_buf[0:CONV_TAIL, :] = qk_buf[T:T + CONV_TAIL, :]
    qk = conv * jax.nn.sigmoid(conv)
    q_all = qk[:, 0:D_MLSTM]
    k_all = qk[:, D_MLSTM:] * (HEAD_DIM ** -0.5)
    v_all = proj(2 * D_MLSTM, 3 * D_MLSTM)
    o_pre = proj(3 * D_MLSTM, 4 * D_MLSTM)

    g_col = jnp.dot(hmix, wg_ref[...], preferred_element_type=F32) + bgc_ref[...]
    g_row = lax.dot_general(wgt_ref[...], hmix, (((1,), (1,)), ((), ())),
                            preferred_element_type=F32) + bgr_ref[...]
    i_col = g_col[:, 0:MLSTM_HEADS]
    i_row = g_row[0:MLSTM_HEADS, :]
    f_col = _log_sigmoid(g_col)
    f_row = _log_sigmoid(g_row)

    rr = lax.broadcasted_iota(jnp.int32, (T, T), 0)
    cc = lax.broadcasted_iota(jnp.int32, (T, T), 1)
    same = (rr // L) == (cc // L)
    tri_lo = jnp.where(same & (cc <= rr), 1.0, 0.0).astype(F32)
    tri_up = jnp.where(same & (rr <= cc), 1.0, 0.0).astype(F32)
    b_col = jnp.dot(tri_lo, f_col, precision=lax.Precision.HIGHEST,
                    preferred_element_type=F32)[:, MLSTM_HEADS:2 * MLSTM_HEADS]
    b_row = jnp.dot(f_row, tri_up, precision=lax.Precision.HIGHEST,
                    preferred_element_type=F32)[MLSTM_HEADS:2 * MLSTM_HEADS, :]

    t_idx = lax.broadcasted_iota(jnp.int32, (L, L), 0)
    s_idx = lax.broadcasted_iota(jnp.int32, (L, L), 1)
    causal = s_idx <= t_idx

    for h in range(MLSTM_HEADS):
        hs = slice(h * HEAD_DIM, (h + 1) * HEAD_DIM)
        ct = c_state[h]
        nrow = n_state[h:h + 1, :]
        m_prev = m_state[h:h + 1, 0:1]
        for c in range(T // L):
            rs = slice(c * L, (c + 1) * L)
            q_c = q_all[rs, hs]
            k_c = k_all[rs, hs]
            v_c = v_all[rs, hs]
            bc = b_col[rs, h:h + 1]
            ic = i_col[rs, h:h + 1]
            br = b_row[h:h + 1, rs]
            ir = i_row[h:h + 1, rs]
            b_last = bc[L - 1:L, :]

            log_d = jnp.where(causal, bc - (br - ir), -jnp.inf)
            m_inter = bc + m_prev
            m_t = jnp.maximum(m_inter, jnp.max(log_d, axis=-1, keepdims=True))
            w_inter = jnp.exp(m_inter - m_t)
            q_cb = q_c.astype(BF16)
            scores = lax.dot_general(q_cb, k_c.astype(BF16), (((1,), (1,)), ((), ())),
                                     preferred_element_type=F32) * jnp.exp(log_d - m_t)
            num = (jnp.dot(scores.astype(BF16), v_c.astype(BF16), preferred_element_type=F32)
                   + w_inter * jnp.dot(q_cb, ct.astype(BF16), preferred_element_type=F32))
            qn = jnp.sum(q_c * nrow, axis=-1, keepdims=True)
            den = jnp.sum(scores, axis=-1, keepdims=True) + w_inter * qn
            h_buf[rs, hs] = num / jnp.maximum(jnp.abs(den), jnp.exp(-m_t))

            m_new = m_t[L - 1:L, :]
            w_state = jnp.exp(b_last - bc + ic - m_new)
            decay = jnp.exp(b_last + m_prev - m_new)
            ct = decay * ct + lax.dot_general(k_c.astype(BF16), (w_state * v_c).astype(BF16),
                                              (((0,), (0,)), ((), ())),
                                              preferred_element_type=F32)
            nrow = decay * nrow + jnp.sum(w_state * k_c, axis=0, keepdims=True)
            m_prev = m_new
        c_state[h] = ct
        n_state[h:h + 1, :] = nrow
        m_state[h:h + 1, :] = jnp.broadcast_to(m_prev, (1, LANES))

    h_m = _rms(h_buf[...]) * gm_ref[...] * jax.nn.sigmoid(o_pre)
    cat_buf[:, 0:D_MLSTM] = h_m.astype(BF16)

    u = proj(4 * D_MLSTM, 4 * D_MLSTM + D_POOL)
    u_buf[POOL_TAIL:POOL_TAIL + T, :] = u
    t_glob = j * T + lax.broadcasted_iota(jnp.int32, (T, 1), 0)
    for g, win in enumerate(POOL_WINDOWS):
        gs = slice(g * POOL_GROUP_DIM, (g + 1) * POOL_GROUP_DIM)
        acc = u[:, gs]
        for back in range(1, win):
            acc = acc + u_buf[pl.ds(POOL_TAIL - back, T), gs]
        count = jnp.minimum(t_glob + 1, win).astype(F32)
        pooled = acc / count - u[:, gs]
        h_p = jnp.dot(pooled.astype(BF16), pw_ref[g], preferred_element_type=F32) * ps_ref[:, gs]
        cat_buf[:, D_MLSTM + g * POOL_GROUP_DIM:D_MLSTM + (g + 1) * POOL_GROUP_DIM] = h_p.astype(BF16)
    u_buf[0:POOL_TAIL, :] = u_buf[T:T + POOL_TAIL, :]

    mixed = jnp.dot(cat_buf[...], wout_ref[...], preferred_element_type=F32)
    o_ref[0] = x + gate1 * mixed


def _mixer(x, mod, norm1_g, w_in, qk_conv_w, qk_conv_b, b_igate, b_fgate, mlstm_norm_g,
           pool_w, pool_scale, w_out):
    bsz, s, d = x.shape
    T = TIME_TILE
    n_main = 4 * D_MLSTM + D_POOL
    w_main = w_in[:, :n_main].astype(BF16)
    w_gate = w_in[:, n_main:]
    wg = jnp.pad(w_gate, ((0, 0), (0, LANES - 2 * MLSTM_HEADS))).astype(BF16)
    wgt = w_gate.T.astype(BF16)
    b_gate = jnp.concatenate([b_igate, b_fgate])
    bgc = jnp.pad(b_gate, (0, LANES - 2 * MLSTM_HEADS)).reshape(1, LANES)
    bgr = b_gate.reshape(2 * MLSTM_HEADS, 1)

    def const(shape):
        return pl.BlockSpec(shape, lambda b, j: (0,) * len(shape), pipeline_mode=pl.Buffered(1))

    return pl.pallas_call(
        _mixer_kernel,
        grid=(bsz, s // T),
        in_specs=[
            pl.BlockSpec((1, T, d), lambda b, j: (b, j, 0)),
            pl.BlockSpec((1, N_MOD, d), lambda b, j: (b, 0, 0)),
            const((1, d)),
            const((d, n_main)),
            const((d, LANES)),
            const((2 * MLSTM_HEADS, d)),
            const((1, LANES)),
            const((2 * MLSTM_HEADS, 1)),
            const((QK_CONV_WIDTH, 2 * D_MLSTM)),
            const((1, 2 * D_MLSTM)),
            const((1, D_MLSTM)),
            const((len(POOL_WINDOWS), POOL_GROUP_DIM, POOL_GROUP_DIM)),
            const((1, D_POOL)),
            const((D_MLSTM + D_POOL, d)),
        ],
        out_specs=pl.BlockSpec((1, T, d), lambda b, j: (b, j, 0)),
        out_shape=jax.ShapeDtypeStruct((bsz, s, d), F32),
        scratch_shapes=[
            pltpu.VMEM((T + CONV_TAIL, 2 * D_MLSTM), F32),
            pltpu.VMEM((T + POOL_TAIL, D_POOL), F32),
            pltpu.VMEM((T, D_MLSTM), F32),
            pltpu.VMEM((T, D_MLSTM + D_POOL), BF16),
            pltpu.VMEM((MLSTM_HEADS, HEAD_DIM, HEAD_DIM), F32),
            pltpu.VMEM((SUBLANES, HEAD_DIM), F32),
            pltpu.VMEM((SUBLANES, LANES), F32),
        ],
        compiler_params=pltpu.CompilerParams(
            dimension_semantics=("arbitrary", "arbitrary"),
            vmem_limit_bytes=VMEM_LIMIT_BYTES),
        name="token_mixer",
    )(x, mod, norm1_g.reshape(1, d), w_main, wg, wgt, bgc, bgr, qk_conv_w,
      qk_conv_b.reshape(1, -1), mlstm_norm_g.reshape(1, -1), pool_w.astype(BF16),
      pool_scale.reshape(1, -1), w_out.astype(BF16))


def _ffn_kernel(x_ref, mod_ref, g2_ref, wup_ref, cw_ref, cb_ref, wdn_ref, gf_ref, o_ref,
                up_buf, tail_buf):
    T = TIME_TILE
    FB = FF_BLOCK
    j = pl.program_id(1)

    @pl.when(j == 0)
    def _():
        tail_buf[...] = jnp.zeros_like(tail_buf)

    x = x_ref[0]
    shift2 = mod_ref[0, 3:4, :]
    scale2 = mod_ref[0, 4:5, :]
    gate2 = mod_ref[0, 5:6, :]
    hff = (_rms(x) * g2_ref[...] * (1.0 + scale2) + shift2).astype(BF16)

    def conv_block(col):
        cs = slice(col, col + FB)
        up_buf[0:CONV_TAIL, :] = tail_buf[:, cs]
        up_buf[CONV_TAIL:CONV_TAIL + T, :] = jnp.dot(hff, wup_ref[:, cs], preferred_element_type=F32)
        tail_buf[:, cs] = up_buf[T:T + CONV_TAIL, :]
        out = cb_ref[:, cs]
        for kk in range(FFN_CONV_WIDTH):
            off = CONV_TAIL - (FFN_CONV_WIDTH - 1) + kk
            out = out + cw_ref[kk:kk + 1, cs] * up_buf[pl.ds(off, T), :]
        return out

    acc = jnp.zeros((T, D_MODEL), F32)
    for blk in range(D_FF // FB):
        g = conv_block(blk * FB)
        val = conv_block(D_FF + blk * FB)
        act = (g * jax.nn.sigmoid(g) * val).astype(BF16)
        acc = acc + jnp.dot(act, wdn_ref[blk * FB:(blk + 1) * FB, :], preferred_element_type=F32)

    o_ref[0] = _rms(x + gate2 * acc) * gf_ref[...]


def _ffn(x1, mod, norm2_g, w_up, ffn_conv_w, ffn_conv_b, w_down, final_norm_g):
    bsz, s, d = x1.shape
    T = TIME_TILE

    def const(shape):
        return pl.BlockSpec(shape, lambda b, j: (0,) * len(shape), pipeline_mode=pl.Buffered(1))

    return pl.pallas_call(
        _ffn_kernel,
        grid=(bsz, s // T),
        in_specs=[
            pl.BlockSpec((1, T, d), lambda b, j: (b, j, 0)),
            pl.BlockSpec((1, N_MOD, d), lambda b, j: (b, 0, 0)),
            const((1, d)),
            const((d, 2 * D_FF)),
            const((FFN_CONV_WIDTH, 2 * D_FF)),
            const((1, 2 * D_FF)),
            const((D_FF, d)),
            const((1, d)),
        ],
        out_specs=pl.BlockSpec((1, T, d), lambda b, j: (b, j, 0)),
        out_shape=jax.ShapeDtypeStruct((bsz, s, d), F32),
        scratch_shapes=[
            pltpu.VMEM((T + CONV_TAIL, FF_BLOCK), F32),
            pltpu.VMEM((CONV_TAIL, 2 * D_FF), F32),
        ],
        compiler_params=pltpu.CompilerParams(
            dimension_semantics=("arbitrary", "arbitrary"),
            vmem_limit_bytes=VMEM_LIMIT_BYTES),
        name="channel_mixer",
    )(x1, mod, norm2_g.reshape(1, d), w_up.astype(BF16), ffn_conv_w,
      ffn_conv_b.reshape(1, -1), w_down.astype(BF16), final_norm_g.reshape(1, d))


def kernel(x, c, w_ada, b_ada, norm1_g, w_in, qk_conv_w, qk_conv_b, b_igate, b_fgate,
           mlstm_norm_g, pool_w, pool_scale, w_out, norm2_g, w_up, ffn_conv_w, ffn_conv_b,
           w_down, final_norm_g):
    bsz = x.shape[0]
    mod = _ada_mod(c, w_ada[0], b_ada[0]).reshape(bsz, N_MOD, D_MODEL)
    x1 = _mixer(x, mod, norm1_g[0], w_in[0], qk_conv_w[0], qk_conv_b[0], b_igate[0], b_fgate[0],
                mlstm_norm_g[0], pool_w[0], pool_scale[0], w_out[0])
    return _ffn(x1, mod, norm2_g[0], w_up[0], ffn_conv_w[0], ffn_conv_b[0], w_down[0], final_norm_g)
```

```python
import jax
import jax.numpy as jnp
from jax import lax
from jax.experimental import pallas as pl
from jax.experimental.pallas import tpu as pltpu

D_MODEL = 1024
MLSTM_HEADS = 4
D_MLSTM = 512
HEAD_DIM = 128
QK_CONV_WIDTH = 4
POOL_WINDOWS = (2, 4, 8, 16)
D_POOL = 512
POOL_GROUP_DIM = 128
D_FF = 2816
FFN_CONV_WIDTH = 3
N_MOD = 6
EPS = 1e-6

SUBLANES = 8
LANES = 128
VMEM_LIMIT_BYTES = 56 * 1024 * 1024

TIME_TILE = 512
MLSTM_CHUNK = 128
FF_BLOCK = 256
POOL_TAIL = 16
CONV_TAIL = SUBLANES
GATE_ROWS = 2 * SUBLANES

F32 = jnp.float32
BF16 = jnp.bfloat16


def _rms(x):
    return x * lax.rsqrt(jnp.mean(x * x, axis=-1, keepdims=True) + EPS)


def _log_sigmoid(x):
    return jnp.minimum(x, 0.0) - jnp.log1p(jnp.exp(-jnp.abs(x)))


def _ada_kernel(c_ref, w_ref, b_ref, o_ref):
    cond = c_ref[...]
    cond = cond * jax.nn.sigmoid(cond)
    o_ref[...] = jnp.dot(cond, w_ref[...], precision=lax.Precision.HIGHEST,
                         preferred_element_type=F32) + b_ref[...]


def _ada_mod(c, w_ada, b_ada):
    bsz = c.shape[0]
    n = w_ada.shape[1]
    blk = D_MODEL
    return pl.pallas_call(
        _ada_kernel,
        grid=(n // blk,),
        in_specs=[pl.BlockSpec((bsz, D_MODEL), lambda i: (0, 0)),
                  pl.BlockSpec((D_MODEL, blk), lambda i: (0, i)),
                  pl.BlockSpec((1, blk), lambda i: (0, i))],
        out_specs=pl.BlockSpec((bsz, blk), lambda i: (0, i)),
        out_shape=jax.ShapeDtypeStruct((bsz, n), F32),
        name="ada_mod",
    )(c, w_ada, b_ada.reshape(1, n))


def _chunk_scan(a, b, axis):
    pos = lax.broadcasted_iota(jnp.int32, a.shape, axis) % MLSTM_CHUNK
    d = 1
    while d < MLSTM_CHUNK:
        keep = pos >= d
        a_prev = jnp.where(keep, pltpu.roll(a, d, axis), 0.0)
        if b is not None:
            b_prev = jnp.where(keep, pltpu.roll(b, d, axis), -jnp.inf)
            b = jnp.maximum(b_prev + a, b)
        a = a_prev + a
        d *= 2
    return a, b


def _mixer_kernel(x_ref, mod_ref, g1_ref, win_ref, wg_ref, wgt_ref, bgc_ref, bgr_ref,
                  cw_ref, cb_ref, gm_ref, pw_ref, ps_ref, wout_ref, o_ref,
                  qk_buf, u_buf, h_buf, cat_buf, c_state, m_state):
    T = TIME_TILE
    L = MLSTM_CHUNK
    NC = T // L
    j = pl.program_id(1)

    @pl.when(j == 0)
    def _():
        qk_buf[0:CONV_TAIL, :] = jnp.zeros((CONV_TAIL, 2 * D_MLSTM), F32)
        u_buf[0:POOL_TAIL, :] = jnp.zeros((POOL_TAIL, D_POOL), F32)
        c_state[...] = jnp.zeros_like(c_state)
        m_state[...] = jnp.zeros_like(m_state)

    x = x_ref[0]
    shift1 = mod_ref[0, 0:1, :]
    scale1 = mod_ref[0, 1:2, :]
    gate1 = mod_ref[0, 2:3, :]
    hmix = (_rms(x) * g1_ref[...] * (1.0 + scale1) + shift1).astype(BF16)

    def proj(lo, hi):
        return jnp.dot(hmix, win_ref[:, lo:hi], preferred_element_type=F32)

    qk_buf[CONV_TAIL:CONV_TAIL + T, :] = proj(0, 2 * D_MLSTM)
    conv = cb_ref[...]
    for kk in range(QK_CONV_WIDTH):
        off = CONV_TAIL - (QK_CONV_WIDTH - 1) + kk
        conv = conv + cw_ref[kk:kk + 1, :] * qk_buf[pl.ds(off, T), :]
    qk_buf[0:CONV_TAIL, :] = qk_buf[T:T + CONV_TAIL, :]
    qk = conv * jax.nn.sigmoid(conv)
    q_all = qk[:, 0:D_MLSTM].astype(BF16)
    k_all = (qk[:, D_MLSTM:] * (HEAD_DIM ** -0.5)).astype(BF16)
    v_all = proj(2 * D_MLSTM, 3 * D_MLSTM)
    o_pre = proj(3 * D_MLSTM, 4 * D_MLSTM)

    g_col = jnp.dot(hmix, wg_ref[...], preferred_element_type=F32) + bgc_ref[...]
    i_col = g_col[:, 0:LANES]
    f_col = _log_sigmoid(g_col[:, LANES:])
    g_row = lax.dot_general(wgt_ref[...], hmix, (((1,), (1,)), ((), ())),
                            preferred_element_type=F32) + bgr_ref[...]
    i_row = g_row[0:SUBLANES, :]
    f_row = _log_sigmoid(g_row[SUBLANES:, :])

    b_col, a_col = _chunk_scan(f_col, i_col, 0)
    b_row, _ = _chunk_scan(f_row, None, 1)
    r_row = i_row - b_row
    m_c = m_state[0:1, :]
    m_prev_rows, m_new_rows, b_last_rows, decays = [], [], [], []
    for c in range(NC):
        b_last = b_col[(c + 1) * L - 1:(c + 1) * L, :]
        m_n = jnp.maximum(m_c + b_last, a_col[(c + 1) * L - 1:(c + 1) * L, :])
        decays.append(jnp.exp(b_last + m_c - m_n))
        m_prev_rows.append(jnp.broadcast_to(m_c, (L, LANES)))
        m_new_rows.append(jnp.broadcast_to(m_n, (L, LANES)))
        b_last_rows.append(jnp.broadcast_to(b_last, (L, LANES)))
        m_c = m_n
    m_state[0:1, :] = m_c
    m_prev = jnp.concatenate(m_prev_rows, axis=0)
    m_new = jnp.concatenate(m_new_rows, axis=0)
    b_last_all = jnp.concatenate(b_last_rows, axis=0)
    m_t = jnp.maximum(b_col + m_prev, a_col)
    e1 = b_col - m_t
    w_inter = jnp.exp(b_col + m_prev - m_t)
    e_negm = jnp.exp(-m_t)
    w_state = jnp.exp(b_last_all - b_col + i_col - m_new)

    t_idx = lax.broadcasted_iota(jnp.int32, (L, L), 0)
    s_idx = lax.broadcasted_iota(jnp.int32, (L, L), 1)
    causal = s_idx <= t_idx
    ones = jnp.ones((L, HEAD_DIM), F32)

    for h in range(MLSTM_HEADS):
        hs = slice(h * HEAD_DIM, (h + 1) * HEAD_DIM)
        ct = c_state[h]
        for c in range(NC):
            rs = slice(c * L, (c + 1) * L)
            q_c = q_all[rs, hs]
            k_c = k_all[rs, hs]
            v_aug = jnp.concatenate([v_all[rs, hs], ones], axis=1)
            p = e1[rs, h:h + 1] + r_row[h:h + 1, rs]
            dm = jnp.exp(jnp.where(causal, p, -jnp.inf))
            scores = lax.dot_general(q_c, k_c, (((1,), (1,)), ((), ())),
                                     preferred_element_type=F32) * dm
            res = (jnp.dot(scores.astype(BF16), v_aug.astype(BF16), preferred_element_type=F32)
                   + w_inter[rs, h:h + 1] * jnp.dot(q_c, ct.astype(BF16), preferred_element_type=F32))
            den = jnp.maximum(jnp.abs(res[:, HEAD_DIM:]), e_negm[rs, h:h + 1])
            h_buf[rs, hs] = res[:, 0:HEAD_DIM] / den
            ct = decays[c][:, h:h + 1] * ct + lax.dot_general(
                k_c, (w_state[rs, h:h + 1] * v_aug).astype(BF16), (((0,), (0,)), ((), ())),
                preferred_element_type=F32)
        c_state[h] = ct

    h_m = _rms(h_buf[...]) * gm_ref[...] * jax.nn.sigmoid(o_pre)
    cat_buf[:, 0:D_MLSTM] = h_m.astype(BF16)

    u = proj(4 * D_MLSTM, 4 * D_MLSTM + D_POOL)
    u_buf[POOL_TAIL:POOL_TAIL + T, :] = u
    t_glob = j * T + lax.broadcasted_iota(jnp.int32, (T, 1), 0)
    for g, win in enumerate(POOL_WINDOWS):
        gs = slice(g * POOL_GROUP_DIM, (g + 1) * POOL_GROUP_DIM)
        acc = u[:, gs]
        for back in range(1, win):
            acc = acc + u_buf[pl.ds(POOL_TAIL - back, T), gs]
        count = jnp.minimum(t_glob + 1, win).astype(F32)
        pooled = acc / count - u[:, gs]
        h_p = jnp.dot(pooled.astype(BF16), pw_ref[g], preferred_element_type=F32) * ps_ref[:, gs]
        cat_buf[:, D_MLSTM + g * POOL_GROUP_DIM:D_MLSTM + (g + 1) * POOL_GROUP_DIM] = h_p.astype(BF16)
    u_buf[0:POOL_TAIL, :] = u_buf[T:T + POOL_TAIL, :]

    mixed = jnp.dot(cat_buf[...], wout_ref[...], preferred_element_type=F32)
    o_ref[0] = x + gate1 * mixed


def _mixer(x, mod, norm1_g, w_in, qk_conv_w, qk_conv_b, b_igate, b_fgate, mlstm_norm_g,
           pool_w, pool_scale, w_out):
    bsz, s, d = x.shape
    T = TIME_TILE
    H = MLSTM_HEADS
    n_main = 4 * D_MLSTM + D_POOL
    w_main = w_in[:, :n_main].astype(BF16)
    w_i = w_in[:, n_main:n_main + H]
    w_f = w_in[:, n_main + H:]
    wg = jnp.concatenate([jnp.pad(w_i, ((0, 0), (0, LANES - H))),
                          jnp.pad(w_f, ((0, 0), (0, LANES - H)))], axis=1).astype(BF16)
    wgt = jnp.concatenate([jnp.pad(w_i.T, ((0, SUBLANES - H), (0, 0))),
                           jnp.pad(w_f.T, ((0, SUBLANES - H), (0, 0)))], axis=0).astype(BF16)
    bgc = jnp.concatenate([jnp.pad(b_igate, (0, LANES - H)),
                           jnp.pad(b_fgate, (0, LANES - H))]).reshape(1, 2 * LANES)
    bgr = jnp.concatenate([jnp.pad(b_igate, (0, SUBLANES - H)),
                           jnp.pad(b_fgate, (0, SUBLANES - H))]).reshape(GATE_ROWS, 1)

    def const(shape):
        return pl.BlockSpec(shape, lambda b, j: (0,) * len(shape), pipeline_mode=pl.Buffered(1))

    return pl.pallas_call(
        _mixer_kernel,
        grid=(bsz, s // T),
        in_specs=[
            pl.BlockSpec((1, T, d), lambda b, j: (b, j, 0)),
            pl.BlockSpec((1, N_MOD, d), lambda b, j: (b, 0, 0)),
            const((1, d)),
            const((d, n_main)),
            const((d, 2 * LANES)),
            const((GATE_ROWS, d)),
            const((1, 2 * LANES)),
            const((GATE_ROWS, 1)),
            const((QK_CONV_WIDTH, 2 * D_MLSTM)),
            const((1, 2 * D_MLSTM)),
            const((1, D_MLSTM)),
            const((len(POOL_WINDOWS), POOL_GROUP_DIM, POOL_GROUP_DIM)),
            const((1, D_POOL)),
            const((D_MLSTM + D_POOL, d)),
        ],
        out_specs=pl.BlockSpec((1, T, d), lambda b, j: (b, j, 0)),
        out_shape=jax.ShapeDtypeStruct((bsz, s, d), F32),
        scratch_shapes=[
            pltpu.VMEM((T + CONV_TAIL, 2 * D_MLSTM), F32),
            pltpu.VMEM((T + POOL_TAIL, D_POOL), F32),
            pltpu.VMEM((T, D_MLSTM), F32),
            pltpu.VMEM((T, D_MLSTM + D_POOL), BF16),
            pltpu.VMEM((MLSTM_HEADS, HEAD_DIM, 2 * HEAD_DIM), F32),
            pltpu.VMEM((SUBLANES, LANES), F32),
        ],
        compiler_params=pltpu.CompilerParams(
            dimension_semantics=("arbitrary", "arbitrary"),
            vmem_limit_bytes=VMEM_LIMIT_BYTES),
        name="token_mixer",
    )(x, mod, norm1_g.reshape(1, d), w_main, wg, wgt, bgc, bgr, qk_conv_w,
      qk_conv_b.reshape(1, -1), mlstm_norm_g.reshape(1, -1), pool_w.astype(BF16),
      pool_scale.reshape(1, -1), w_out.astype(BF16))


def _ffn_kernel(x_ref, mod_ref, g2_ref, wup_ref, cw_ref, cb_ref, wdn_ref, gf_ref, o_ref,
                up_buf, tail_buf):
    T = TIME_TILE
    FB = FF_BLOCK
    j = pl.program_id(1)

    @pl.when(j == 0)
    def _():
        tail_buf[...] = jnp.zeros_like(tail_buf)

    x = x_ref[0]
    shift2 = mod_ref[0, 3:4, :]
    scale2 = mod_ref[0, 4:5, :]
    gate2 = mod_ref[0, 5:6, :]
    hff = (_rms(x) * g2_ref[...] * (1.0 + scale2) + shift2).astype(BF16)

    def conv_block(col):
        cs = slice(col, col + FB)
        up_buf[0:CONV_TAIL, :] = tail_buf[:, cs]
        up_buf[CONV_TAIL:CONV_TAIL + T, :] = jnp.dot(hff, wup_ref[:, cs], preferred_element_type=F32)
        tail_buf[:, cs] = up_buf[T:T + CONV_TAIL, :]
        out = cb_ref[:, cs]
        for kk in range(FFN_CONV_WIDTH):
            off = CONV_TAIL - (FFN_CONV_WIDTH - 1) + kk
            out = out + cw_ref[kk:kk + 1, cs] * up_buf[pl.ds(off, T), :]
        return out

    acc = jnp.zeros((T, D_MODEL), F32)
    for blk in range(D_FF // FB):
        g = conv_block(blk * FB)
        val = conv_block(D_FF + blk * FB)
        act = (g * jax.nn.sigmoid(g) * val).astype(BF16)
        acc = acc + jnp.dot(act, wdn_ref[blk * FB:(blk + 1) * FB, :], preferred_element_type=F32)

    o_ref[0] = _rms(x + gate2 * acc) * gf_ref[...]


def _ffn(x1, mod, norm2_g, w_up, ffn_conv_w, ffn_conv_b, w_down, final_norm_g):
    bsz, s, d = x1.shape
    T = TIME_TILE

    def const(shape):
        return pl.BlockSpec(shape, lambda b, j: (0,) * len(shape), pipeline_mode=pl.Buffered(1))

    return pl.pallas_call(
        _ffn_kernel,
        grid=(bsz, s // T),
        in_specs=[
            pl.BlockSpec((1, T, d), lambda b, j: (b, j, 0)),
            pl.BlockSpec((1, N_MOD, d), lambda b, j: (b, 0, 0)),
            const((1, d)),
            const((d, 2 * D_FF)),
            const((FFN_CONV_WIDTH, 2 * D_FF)),
            const((1, 2 * D_FF)),
            const((D_FF, d)),
            const((1, d)),
        ],
        out_specs=pl.BlockSpec((1, T, d), lambda b, j: (b, j, 0)),
        out_shape=jax.ShapeDtypeStruct((bsz, s, d), F32),
        scratch_shapes=[
            pltpu.VMEM((T + CONV_TAIL, FF_BLOCK), F32),
            pltpu.VMEM((CONV_TAIL, 2 * D_FF), F32),
        ],
        compiler_params=pltpu.CompilerParams(
            dimension_semantics=("arbitrary", "arbitrary"),
            vmem_limit_bytes=VMEM_LIMIT_BYTES),
        name="channel_mixer",
    )(x1, mod, norm2_g.reshape(1, d), w_up.astype(BF16), ffn_conv_w,
      ffn_conv_b.reshape(1, -1), w_down.astype(BF16), final_norm_g.reshape(1, d))


def kernel(x, c, w_ada, b_ada, norm1_g, w_in, qk_conv_w, qk_conv_b, b_igate, b_fgate,
           mlstm_norm_g, pool_w, pool_scale, w_out, norm2_g, w_up, ffn_conv_w, ffn_conv_b,
           w_down, final_norm_g):
    bsz = x.shape[0]
    mod = _ada_mod(c, w_ada[0], b_ada[0]).reshape(bsz, N_MOD, D_MODEL)
    x1 = _mixer(x, mod, norm1_g[0], w_in[0], qk_conv_w[0], qk_conv_b[0], b_igate[0], b_fgate[0],
                mlstm_norm_g[0], pool_w[0], pool_scale[0], w_out[0])
    return _ffn(x1, mod, norm2_g[0], w_up[0], ffn_conv_w[0], ffn_conv_b[0], w_down[0], final_norm_g)
```

```python
import jax
import jax.numpy as jnp
from jax import lax
from jax.experimental import pallas as pl
from jax.experimental.pallas import tpu as pltpu

D_MODEL = 1024
MLSTM_HEADS = 4
D_MLSTM = 512
HEAD_DIM = 128
QK_CONV_WIDTH = 4
POOL_WINDOWS = (2, 4, 8, 16)
D_POOL = 512
POOL_GROUP_DIM = 128
D_FF = 2816
FFN_CONV_WIDTH = 3
N_MOD = 6
EPS = 1e-6

SUBLANES = 8
LANES = 128
VMEM_LIMIT_BYTES = 56 * 1024 * 1024

TIME_TILE = 512
STRAND = TIME_TILE // SUBLANES
MLSTM_CHUNK = 128
FF_BLOCK = 256
POOL_TAIL = 16
CONV_TAIL = SUBLANES
GATE_ROWS = 2 * SUBLANES

F32 = jnp.float32
BF16 = jnp.bfloat16


def _rms(x):
    return x * lax.rsqrt(jnp.mean(x * x, axis=-1, keepdims=True) + EPS)


def _log_sigmoid(x):
    return jnp.minimum(x, 0.0) - jnp.log1p(jnp.exp(-jnp.abs(x)))


def _ada_kernel(c_ref, w_ref, b_ref, o_ref):
    cond = c_ref[...]
    cond = cond * jax.nn.sigmoid(cond)
    o_ref[...] = jnp.dot(cond, w_ref[...], precision=lax.Precision.HIGHEST,
                         preferred_element_type=F32) + b_ref[...]


def _ada_mod(c, w_ada, b_ada):
    bsz = c.shape[0]
    n = w_ada.shape[1]
    blk = D_MODEL
    return pl.pallas_call(
        _ada_kernel,
        grid=(n // blk,),
        in_specs=[pl.BlockSpec((bsz, D_MODEL), lambda i: (0, 0)),
                  pl.BlockSpec((D_MODEL, blk), lambda i: (0, i)),
                  pl.BlockSpec((1, blk), lambda i: (0, i))],
        out_specs=pl.BlockSpec((bsz, blk), lambda i: (0, i)),
        out_shape=jax.ShapeDtypeStruct((bsz, n), F32),
        name="ada_mod",
    )(c, w_ada, b_ada.reshape(1, n))


def _chunk_scan(a, b, axis):
    pos = lax.broadcasted_iota(jnp.int32, a.shape, axis) % MLSTM_CHUNK
    d = 1
    while d < MLSTM_CHUNK:
        keep = pos >= d
        a_prev = jnp.where(keep, pltpu.roll(a, d, axis), 0.0)
        if b is not None:
            b_prev = jnp.where(keep, pltpu.roll(b, d, axis), -jnp.inf)
            b = jnp.maximum(b_prev + a, b)
        a = a_prev + a
        d *= 2
    return a, b


def _mixer_kernel(x_ref, mod_ref, g1_ref, win_ref, wg_ref, wgt_ref, bgc_ref, bgr_ref,
                  cw_ref, cb_ref, gm_ref, pw_ref, ps_ref, wout_ref, o_ref,
                  qk_buf, u_buf, h_buf, cat_buf, c_state, m_state):
    T = TIME_TILE
    L = MLSTM_CHUNK
    NC = T // L
    j = pl.program_id(1)

    @pl.when(j == 0)
    def _():
        qk_buf[0:CONV_TAIL, :] = jnp.zeros((CONV_TAIL, 2 * D_MLSTM), F32)
        u_buf[0:POOL_TAIL, :] = jnp.zeros((POOL_TAIL, D_POOL), F32)
        c_state[...] = jnp.zeros_like(c_state)
        m_state[...] = jnp.zeros_like(m_state)

    x = x_ref[0]
    shift1 = mod_ref[0, 0:1, :]
    scale1 = mod_ref[0, 1:2, :]
    gate1 = mod_ref[0, 2:3, :]
    hmix = (_rms(x) * g1_ref[...] * (1.0 + scale1) + shift1).astype(BF16)

    def proj(lo, hi):
        return jnp.dot(hmix, win_ref[:, lo:hi], preferred_element_type=F32)

    qk_buf[CONV_TAIL:CONV_TAIL + T, :] = proj(0, 2 * D_MLSTM)
    conv = cb_ref[...]
    for kk in range(QK_CONV_WIDTH):
        off = CONV_TAIL - (QK_CONV_WIDTH - 1) + kk
        conv = conv + cw_ref[kk:kk + 1, :] * qk_buf[pl.ds(off, T), :]
    qk_buf[0:CONV_TAIL, :] = qk_buf[T:T + CONV_TAIL, :]
    qk = conv * jax.nn.sigmoid(conv)
    q_all = qk[:, 0:D_MLSTM].astype(BF16)
    k_all = (qk[:, D_MLSTM:] * (HEAD_DIM ** -0.5)).astype(BF16)
    v_all = proj(2 * D_MLSTM, 3 * D_MLSTM)
    o_pre = proj(3 * D_MLSTM, 4 * D_MLSTM)

    g_col = jnp.dot(hmix, wg_ref[...], preferred_element_type=F32) + bgc_ref[...]
    i_col = g_col[:, 0:LANES]
    f_col = _log_sigmoid(g_col[:, LANES:])
    g_row = lax.dot_general(wgt_ref[...], hmix, (((1,), (1,)), ((), ())),
                            preferred_element_type=F32) + bgr_ref[...]
    i_row = g_row[0:SUBLANES, :]
    f_row = _log_sigmoid(g_row[SUBLANES:, :])

    b_col, a_col = _chunk_scan(f_col, i_col, 0)
    b_row, _ = _chunk_scan(f_row, None, 1)
    r_row = i_row - b_row
    m_c = m_state[0:1, :]
    m_prev_rows, m_new_rows, b_last_rows, decays = [], [], [], []
    for c in range(NC):
        b_last = b_col[(c + 1) * L - 1:(c + 1) * L, :]
        m_n = jnp.maximum(m_c + b_last, a_col[(c + 1) * L - 1:(c + 1) * L, :])
        decays.append(jnp.exp(b_last + m_c - m_n))
        m_prev_rows.append(jnp.broadcast_to(m_c, (L, LANES)))
        m_new_rows.append(jnp.broadcast_to(m_n, (L, LANES)))
        b_last_rows.append(jnp.broadcast_to(b_last, (L, LANES)))
        m_c = m_n
    m_state[0:1, :] = m_c
    m_prev = jnp.concatenate(m_prev_rows, axis=0)
    m_new = jnp.concatenate(m_new_rows, axis=0)
    b_last_all = jnp.concatenate(b_last_rows, axis=0)
    m_t = jnp.maximum(b_col + m_prev, a_col)
    e1 = b_col - m_t
    w_inter = jnp.exp(b_col + m_prev - m_t)
    e_negm = jnp.exp(-m_t)
    w_state = jnp.exp(b_last_all - b_col + i_col - m_new)

    t_idx = lax.broadcasted_iota(jnp.int32, (L, L), 0)
    s_idx = lax.broadcasted_iota(jnp.int32, (L, L), 1)
    causal = s_idx <= t_idx
    ones = jnp.ones((L, HEAD_DIM), F32)

    for h in range(MLSTM_HEADS):
        hs = slice(h * HEAD_DIM, (h + 1) * HEAD_DIM)
        ct = c_state[h]
        for c in range(NC):
            rs = slice(c * L, (c + 1) * L)
            q_c = q_all[rs, hs]
            k_c = k_all[rs, hs]
            v_aug = jnp.concatenate([v_all[rs, hs], ones], axis=1)
            p = e1[rs, h:h + 1] + r_row[h:h + 1, rs]
            dm = jnp.exp(jnp.where(causal, p, -jnp.inf))
            scores = lax.dot_general(q_c, k_c, (((1,), (1,)), ((), ())),
                                     preferred_element_type=F32) * dm
            res = (jnp.dot(scores.astype(BF16), v_aug.astype(BF16), preferred_element_type=F32)
                   + w_inter[rs, h:h + 1] * jnp.dot(q_c, ct.astype(BF16), preferred_element_type=F32))
            den = jnp.maximum(jnp.abs(res[:, HEAD_DIM:]), e_negm[rs, h:h + 1])
            h_buf[rs, hs] = res[:, 0:HEAD_DIM] / den
            ct = decays[c][:, h:h + 1] * ct + lax.dot_general(
                k_c, (w_state[rs, h:h + 1] * v_aug).astype(BF16), (((0,), (0,)), ((), ())),
                preferred_element_type=F32)
        c_state[h] = ct

    h_m = _rms(h_buf[...]) * gm_ref[...] * jax.nn.sigmoid(o_pre)
    cat_buf[:, 0:D_MLSTM] = h_m.astype(BF16)

    u = proj(4 * D_MLSTM, 4 * D_MLSTM + D_POOL)
    u_buf[POOL_TAIL:POOL_TAIL + T, :] = u
    t_glob = j * T + lax.broadcasted_iota(jnp.int32, (T, 1), 0)
    for g, win in enumerate(POOL_WINDOWS):
        gs = slice(g * POOL_GROUP_DIM, (g + 1) * POOL_GROUP_DIM)
        acc = u[:, gs]
        for back in range(1, win):
            acc = acc + u_buf[pl.ds(POOL_TAIL - back, T), gs]
        count = jnp.minimum(t_glob + 1, win).astype(F32)
        pooled = acc / count - u[:, gs]
        h_p = jnp.dot(pooled.astype(BF16), pw_ref[g], preferred_element_type=F32) * ps_ref[:, gs]
        cat_buf[:, D_MLSTM + g * POOL_GROUP_DIM:D_MLSTM + (g + 1) * POOL_GROUP_DIM] = h_p.astype(BF16)
    u_buf[0:POOL_TAIL, :] = u_buf[T:T + POOL_TAIL, :]

    mixed = jnp.dot(cat_buf[...], wout_ref[...], preferred_element_type=F32)
    x1 = x + gate1 * mixed
    for slab in range(D_MODEL // LANES):
        for s in range(SUBLANES):
            o_ref[0, slab, pl.ds(s, STRAND, stride=SUBLANES), :] = (
                x1[s * STRAND:(s + 1) * STRAND, slab * LANES:(slab + 1) * LANES])


def _mixer(x, mod, norm1_g, w_in, qk_conv_w, qk_conv_b, b_igate, b_fgate, mlstm_norm_g,
           pool_w, pool_scale, w_out):
    bsz, s, d = x.shape
    T = TIME_TILE
    H = MLSTM_HEADS
    n_main = 4 * D_MLSTM + D_POOL
    w_main = w_in[:, :n_main].astype(BF16)
    w_i = w_in[:, n_main:n_main + H]
    w_f = w_in[:, n_main + H:]
    wg = jnp.concatenate([jnp.pad(w_i, ((0, 0), (0, LANES - H))),
                          jnp.pad(w_f, ((0, 0), (0, LANES - H)))], axis=1).astype(BF16)
    wgt = jnp.concatenate([jnp.pad(w_i.T, ((0, SUBLANES - H), (0, 0))),
                           jnp.pad(w_f.T, ((0, SUBLANES - H), (0, 0)))], axis=0).astype(BF16)
    bgc = jnp.concatenate([jnp.pad(b_igate, (0, LANES - H)),
                           jnp.pad(b_fgate, (0, LANES - H))]).reshape(1, 2 * LANES)
    bgr = jnp.concatenate([jnp.pad(b_igate, (0, SUBLANES - H)),
                           jnp.pad(b_fgate, (0, SUBLANES - H))]).reshape(GATE_ROWS, 1)

    def const(shape):
        return pl.BlockSpec(shape, lambda b, j: (0,) * len(shape), pipeline_mode=pl.Buffered(1))

    return pl.pallas_call(
        _mixer_kernel,
        grid=(bsz, s // T),
        in_specs=[
            pl.BlockSpec((1, T, d), lambda b, j: (b, j, 0)),
            pl.BlockSpec((1, N_MOD, d), lambda b, j: (b, 0, 0)),
            const((1, d)),
            const((d, n_main)),
            const((d, 2 * LANES)),
            const((GATE_ROWS, d)),
            const((1, 2 * LANES)),
            const((GATE_ROWS, 1)),
            const((QK_CONV_WIDTH, 2 * D_MLSTM)),
            const((1, 2 * D_MLSTM)),
            const((1, D_MLSTM)),
            const((len(POOL_WINDOWS), POOL_GROUP_DIM, POOL_GROUP_DIM)),
            const((1, D_POOL)),
            const((D_MLSTM + D_POOL, d)),
        ],
        out_specs=pl.BlockSpec((1, d // LANES, T, LANES), lambda b, j: (b, 0, j, 0)),
        out_shape=jax.ShapeDtypeStruct((bsz, d // LANES, s, LANES), F32),
        scratch_shapes=[
            pltpu.VMEM((T + CONV_TAIL, 2 * D_MLSTM), F32),
            pltpu.VMEM((T + POOL_TAIL, D_POOL), F32),
            pltpu.VMEM((T, D_MLSTM), F32),
            pltpu.VMEM((T, D_MLSTM + D_POOL), BF16),
            pltpu.VMEM((MLSTM_HEADS, HEAD_DIM, 2 * HEAD_DIM), F32),
            pltpu.VMEM((SUBLANES, LANES), F32),
        ],
        compiler_params=pltpu.CompilerParams(
            dimension_semantics=("arbitrary", "arbitrary"),
            vmem_limit_bytes=VMEM_LIMIT_BYTES),
        name="token_mixer",
    )(x, mod, norm1_g.reshape(1, d), w_main, wg, wgt, bgc, bgr, qk_conv_w,
      qk_conv_b.reshape(1, -1), mlstm_norm_g.reshape(1, -1), pool_w.astype(BF16),
      pool_scale.reshape(1, -1), w_out.astype(BF16))


def _ffn_kernel(x_ref, mod_ref, g2_ref, wup_ref, cw_ref, cb_ref, wdn_ref, gf_ref, o_ref,
                act_buf, out_buf, tail_buf):
    T = TIME_TILE
    FB = FF_BLOCK
    G = SUBLANES
    j = pl.program_id(1)

    @pl.when(j == 0)
    def _():
        tail_buf[...] = jnp.zeros_like(tail_buf)

    x = jnp.concatenate([x_ref[0, k] for k in range(D_MODEL // LANES)], axis=1)
    shift2 = mod_ref[0, 3:4, :]
    scale2 = mod_ref[0, 4:5, :]
    gate2 = mod_ref[0, 5:6, :]
    hff = (_rms(x) * g2_ref[...] * (1.0 + scale2) + shift2).astype(BF16)
    last_sublane = lax.broadcasted_iota(jnp.int32, (G, FB), 0) == G - 1

    def conv_block(col):
        cs = slice(col, col + FB)
        y = jnp.dot(hff, wup_ref[:, cs], preferred_element_type=F32)
        f1 = pltpu.roll(jnp.where(last_sublane, tail_buf[G:2 * G, cs], y[T - G:T]), 1, 0)
        f2 = pltpu.roll(jnp.where(last_sublane, tail_buf[0:G, cs], y[T - 2 * G:T - G]), 1, 0)
        tail_buf[:, cs] = y[T - 2 * G:T]
        y1 = jnp.concatenate([f1, y[0:T - G]], axis=0)
        y2 = jnp.concatenate([f2, f1, y[0:T - 2 * G]], axis=0)
        return cb_ref[:, cs] + cw_ref[0:1, cs] * y2 + cw_ref[1:2, cs] * y1 + cw_ref[2:3, cs] * y

    for blk in range(D_FF // FB):
        g = conv_block(blk * FB)
        val = conv_block(D_FF + blk * FB)
        act_buf[:, blk * FB:(blk + 1) * FB] = (g * jax.nn.sigmoid(g) * val).astype(BF16)

    y = jnp.dot(act_buf[...], wdn_ref[...], preferred_element_type=F32)
    res = _rms(x + gate2 * y) * gf_ref[...]
    for slab in range(D_MODEL // LANES):
        out_buf[slab] = res[:, slab * LANES:(slab + 1) * LANES]
    for slab in range(D_MODEL // LANES):
        for s in range(G):
            o_ref[0, s * STRAND:(s + 1) * STRAND, slab * LANES:(slab + 1) * LANES] = (
                out_buf[slab, pl.ds(s, STRAND, stride=G), :])


def _ffn(x1, mod, norm2_g, w_up, ffn_conv_w, ffn_conv_b, w_down, final_norm_g):
    bsz, n_slab, s, _ = x1.shape
    d = D_MODEL
    T = TIME_TILE

    def const(shape):
        return pl.BlockSpec(shape, lambda b, j: (0,) * len(shape), pipeline_mode=pl.Buffered(1))

    return pl.pallas_call(
        _ffn_kernel,
        grid=(bsz, s // T),
        in_specs=[
            pl.BlockSpec((1, n_slab, T, LANES), lambda b, j: (b, 0, j, 0)),
            pl.BlockSpec((1, N_MOD, d), lambda b, j: (b, 0, 0)),
            const((1, d)),
            const((d, 2 * D_FF)),
            const((FFN_CONV_WIDTH, 2 * D_FF)),
            const((1, 2 * D_FF)),
            const((D_FF, d)),
            const((1, d)),
        ],
        out_specs=pl.BlockSpec((1, T, d), lambda b, j: (b, j, 0)),
        out_shape=jax.ShapeDtypeStruct((bsz, s, d), F32),
        scratch_shapes=[
            pltpu.VMEM((T, D_FF), BF16),
            pltpu.VMEM((n_slab, T, LANES), F32),
            pltpu.VMEM((2 * SUBLANES, 2 * D_FF), F32),
        ],
        compiler_params=pltpu.CompilerParams(
            dimension_semantics=("arbitrary", "arbitrary"),
            vmem_limit_bytes=VMEM_LIMIT_BYTES),
        name="channel_mixer",
    )(x1, mod, norm2_g.reshape(1, d), w_up.astype(BF16), ffn_conv_w,
      ffn_conv_b.reshape(1, -1), w_down.astype(BF16), final_norm_g.reshape(1, d))


def kernel(x, c, w_ada, b_ada, norm1_g, w_in, qk_conv_w, qk_conv_b, b_igate, b_fgate,
           mlstm_norm_g, pool_w, pool_scale, w_out, norm2_g, w_up, ffn_conv_w, ffn_conv_b,
           w_down, final_norm_g):
    bsz = x.shape[0]
    mod = _ada_mod(c, w_ada[0], b_ada[0]).reshape(bsz, N_MOD, D_MODEL)
    x1 = _mixer(x, mod, norm1_g[0], w_in[0], qk_conv_w[0], qk_conv_b[0], b_igate[0], b_fgate[0],
                mlstm_norm_g[0], pool_w[0], pool_scale[0], w_out[0])
    return _ffn(x1, mod, norm2_g[0], w_up[0], ffn_conv_w[0], ffn_conv_b[0], w_down[0], final_norm_g)
```

```python
import jax
import jax.numpy as jnp
from jax import lax
from jax.experimental import pallas as pl
from jax.experimental.pallas import tpu as pltpu

D_MODEL = 1024
MLSTM_HEADS = 4
D_MLSTM = 512
HEAD_DIM = 128
QK_CONV_WIDTH = 4
POOL_WINDOWS = (2, 4, 8, 16)
D_POOL = 512
POOL_GROUP_DIM = 128
D_FF = 2816
FFN_CONV_WIDTH = 3
N_MOD = 6
EPS = 1e-6

SUBLANES = 8
LANES = 128
VMEM_LIMIT_BYTES = 56 * 1024 * 1024

TIME_TILE = 512
STRAND = TIME_TILE // SUBLANES
CHUNK_STRANDS = 4
MLSTM_CHUNK = CHUNK_STRANDS * STRAND
FF_BLOCK = 256
N_SLAB = D_MODEL // LANES

F32 = jnp.float32
BF16 = jnp.bfloat16


def _rms(x):
    return x * lax.rsqrt(jnp.mean(x * x, axis=-1, keepdims=True) + EPS)


def _log_sigmoid(x):
    return jnp.minimum(x, 0.0) - jnp.log1p(jnp.exp(-jnp.abs(x)))


def _strand_rows(s):
    return pl.ds(s, STRAND, stride=SUBLANES)


def _history(y, tail_ref, cols, n):
    T, W = y.shape
    G = SUBLANES
    last_sublane = lax.broadcasted_iota(jnp.int32, (G, W), 0) == G - 1
    groups = []
    for m in range(n, 0, -1):
        cur = y[T - G * m:T - G * (m - 1)]
        prev = tail_ref[(n - m) * G:(n - m + 1) * G, cols]
        groups.append(pltpu.roll(jnp.where(last_sublane, prev, cur), 1, 0))
    tail_ref[:, cols] = y[T - G * n:T]
    return jnp.concatenate(groups + [y], axis=0)


def _ada_kernel(c_ref, w_ref, b_ref, o_ref):
    cond = c_ref[...]
    cond = cond * jax.nn.sigmoid(cond)
    o_ref[...] = jnp.dot(cond, w_ref[...], precision=lax.Precision.HIGHEST,
                         preferred_element_type=F32) + b_ref[...]


def _ada_mod(c, w_ada, b_ada):
    bsz = c.shape[0]
    n = w_ada.shape[1]
    blk = D_MODEL
    return pl.pallas_call(
        _ada_kernel,
        grid=(n // blk,),
        in_specs=[pl.BlockSpec((bsz, D_MODEL), lambda i: (0, 0)),
                  pl.BlockSpec((D_MODEL, blk), lambda i: (0, i)),
                  pl.BlockSpec((1, blk), lambda i: (0, i))],
        out_specs=pl.BlockSpec((bsz, blk), lambda i: (0, i)),
        out_shape=jax.ShapeDtypeStruct((bsz, n), F32),
        name="ada_mod",
    )(c, w_ada, b_ada.reshape(1, n))


SC_E1, SC_W_INTER, SC_E_NEGM, SC_W_STATE, SC_R = range(5)


def _mixer_kernel(x_ref, mod_ref, g1_ref, win_ref, wg_ref, bg_ref, cw_ref, cb_ref, gm_ref,
                  pw_ref, ps_ref, wout_ref, o_ref,
                  xs_buf, qkv_buf, sc_buf, h_buf, cat_buf, qk_tail, u_tail, c_state, m_state):
    T = TIME_TILE
    G = SUBLANES
    NG = T // G
    NC = T // MLSTM_CHUNK
    H = MLSTM_HEADS
    j = pl.program_id(1)

    @pl.when(j == 0)
    def _():
        qk_tail[...] = jnp.zeros_like(qk_tail)
        u_tail[...] = jnp.zeros_like(u_tail)
        c_state[...] = jnp.zeros_like(c_state)
        m_state[...] = jnp.zeros_like(m_state)

    for slab in range(N_SLAB):
        for s in range(G):
            xs_buf[slab, _strand_rows(s), :] = (
                x_ref[0, s * STRAND:(s + 1) * STRAND, slab * LANES:(slab + 1) * LANES])
    x = jnp.concatenate([xs_buf[k] for k in range(N_SLAB)], axis=1)

    shift1 = mod_ref[0, 0:1, :]
    scale1 = mod_ref[0, 1:2, :]
    gate1 = mod_ref[0, 2:3, :]
    hmix = (_rms(x) * g1_ref[...] * (1.0 + scale1) + shift1).astype(BF16)

    def proj(lo, hi):
        return jnp.dot(hmix, win_ref[:, lo:hi], preferred_element_type=F32)

    n_back = QK_CONV_WIDTH - 1
    qk_hist = _history(proj(0, 2 * D_MLSTM), qk_tail, slice(None), n_back)
    conv = cb_ref[...]
    for kk in range(QK_CONV_WIDTH):
        conv = conv + cw_ref[kk:kk + 1, :] * qk_hist[kk * G:kk * G + T]
    qk = conv * jax.nn.sigmoid(conv)
    v_all = proj(2 * D_MLSTM, 3 * D_MLSTM)
    for h in range(H):
        hs = slice(h * HEAD_DIM, (h + 1) * HEAD_DIM)
        qkv_buf[h] = qk[:, hs]
        qkv_buf[H + h] = qk[:, D_MLSTM + h * HEAD_DIM:D_MLSTM + (h + 1) * HEAD_DIM] * (HEAD_DIM ** -0.5)
        qkv_buf[2 * H + h] = v_all[:, hs]
    o_pre = proj(3 * D_MLSTM, 4 * D_MLSTM)

    g_col = jnp.dot(hmix, wg_ref[...], preferred_element_type=F32) + bg_ref[...]
    i_col = g_col[:, 0:LANES]
    f_col = _log_sigmoid(g_col[:, LANES:])

    a, b = f_col, i_col
    d = 1
    while d < STRAND:
        a_prev = jnp.concatenate([jnp.zeros((G * d, LANES), F32), a[0:T - G * d]], axis=0)
        b_prev = jnp.concatenate([jnp.full((G * d, LANES), -jnp.inf, F32), b[0:T - G * d]], axis=0)
        b = jnp.maximum(b_prev + a, b)
        a = a_prev + a
        d *= 2
    sub = lax.broadcasted_iota(jnp.int32, (G, LANES), 0)
    lv = 1
    while lv < CHUNK_STRANDS:
        pos = sub % (2 * lv)
        a_carry = jnp.zeros((G, LANES), F32)
        b_carry = jnp.full((G, LANES), -jnp.inf, F32)
        for off in range(1, lv + 1):
            take = pos == lv - 1 + off
            a_carry = jnp.where(take, pltpu.roll(a[T - G:T], off, 0), a_carry)
            b_carry = jnp.where(take, pltpu.roll(b[T - G:T], off, 0), b_carry)
        b = jnp.maximum(pltpu.repeat(b_carry, NG, 0) + a, b)
        a = a + pltpu.repeat(a_carry, NG, 0)
        lv *= 2
    b_col = a
    a_col = b
    b_end = b_col[T - G:T]
    a_end = a_col[T - G:T]
    m_c = m_state[0:1, :]
    m_prev = jnp.zeros((G, LANES), F32)
    m_new = jnp.zeros((G, LANES), F32)
    b_last = jnp.zeros((G, LANES), F32)
    decays = []
    for c in range(NC):
        row = CHUNK_STRANDS * (c + 1) - 1
        b_sum = b_end[row:row + 1, :]
        m_n = jnp.maximum(m_c + b_sum, a_end[row:row + 1, :])
        decays.append(jnp.exp(b_sum + m_c - m_n))
        in_chunk = (sub // CHUNK_STRANDS) == c
        m_prev = jnp.where(in_chunk, m_c, m_prev)
        m_new = jnp.where(in_chunk, m_n, m_new)
        b_last = jnp.where(in_chunk, b_sum, b_last)
        m_c = m_n
    m_state[0:1, :] = m_c
    m_prev = pltpu.repeat(m_prev, NG, 0)
    m_new = pltpu.repeat(m_new, NG, 0)
    b_last = pltpu.repeat(b_last, NG, 0)
    m_t = jnp.maximum(b_col + m_prev, a_col)
    sc_buf[SC_E1] = b_col - m_t
    sc_buf[SC_W_INTER] = jnp.exp(b_col + m_prev - m_t)
    sc_buf[SC_E_NEGM] = jnp.exp(-m_t)
    sc_buf[SC_W_STATE] = jnp.exp(b_last - b_col + i_col - m_new)
    sc_buf[SC_R] = i_col - b_col

    t_idx = lax.broadcasted_iota(jnp.int32, (MLSTM_CHUNK, MLSTM_CHUNK), 0)
    s_idx = lax.broadcasted_iota(jnp.int32, (MLSTM_CHUNK, MLSTM_CHUNK), 1)
    causal = s_idx <= t_idx
    ones = jnp.ones((MLSTM_CHUNK, HEAD_DIM), F32)

    states = [c_state[h] for h in range(H)]
    for c in range(NC):
        rows = [_strand_rows(CHUNK_STRANDS * c + k) for k in range(CHUNK_STRANDS)]

        def gather(ref, k):
            return jnp.concatenate([ref[k, r, :] for r in rows], axis=0)

        e1 = gather(sc_buf, SC_E1)
        w_inter = gather(sc_buf, SC_W_INTER)
        e_negm = gather(sc_buf, SC_E_NEGM)
        w_state = gather(sc_buf, SC_W_STATE)
        r_t = gather(sc_buf, SC_R).T
        for h in range(H):
            q_c = gather(qkv_buf, h).astype(BF16)
            k_c = gather(qkv_buf, H + h).astype(BF16)
            v_aug = jnp.concatenate([gather(qkv_buf, 2 * H + h), ones], axis=1)
            p = e1[:, h:h + 1] + r_t[h:h + 1, :]
            dm = jnp.exp(jnp.where(causal, p, -jnp.inf))
            scores = lax.dot_general(q_c, k_c, (((1,), (1,)), ((), ())),
                                     preferred_element_type=F32) * dm
            res = (jnp.dot(scores.astype(BF16), v_aug.astype(BF16), preferred_element_type=F32)
                   + w_inter[:, h:h + 1] * jnp.dot(q_c, states[h].astype(BF16),
                                                   preferred_element_type=F32))
            den = jnp.maximum(jnp.abs(res[:, HEAD_DIM:]), e_negm[:, h:h + 1])
            h_c = res[:, 0:HEAD_DIM] / den
            for k, r in enumerate(rows):
                h_buf[h, r, :] = h_c[k * STRAND:(k + 1) * STRAND]
            states[h] = decays[c][:, h:h + 1] * states[h] + lax.dot_general(
                k_c, (w_state[:, h:h + 1] * v_aug).astype(BF16), (((0,), (0,)), ((), ())),
                preferred_element_type=F32)
    for h in range(H):
        c_state[h] = states[h]

    h_all = jnp.concatenate([h_buf[h] for h in range(H)], axis=1)
    h_m = _rms(h_all) * gm_ref[...] * jax.nn.sigmoid(o_pre)
    cat_buf[:, 0:D_MLSTM] = h_m.astype(BF16)

    u = proj(4 * D_MLSTM, 4 * D_MLSTM + D_POOL)
    row = lax.broadcasted_iota(jnp.int32, (T, 1), 0)
    t_glob = j * T + (row % G) * STRAND + row // G
    n_tail = max(POOL_WINDOWS) - 1
    for g, win in enumerate(POOL_WINDOWS):
        gs = slice(g * POOL_GROUP_DIM, (g + 1) * POOL_GROUP_DIM)
        u_g = u[:, gs]
        acc = _history(u_g, u_tail.at[(n_tail - (win - 1)) * G:n_tail * G, :], gs, win - 1)
        d = 1
        while d < win:
            acc = acc[G * d:] + acc[:acc.shape[0] - G * d]
            d *= 2
        count = jnp.minimum(t_glob + 1, win).astype(F32)
        pooled = acc / count - u_g
        h_p = jnp.dot(pooled.astype(BF16), pw_ref[g], preferred_element_type=F32) * ps_ref[:, gs]
        cat_buf[:, D_MLSTM + g * POOL_GROUP_DIM:D_MLSTM + (g + 1) * POOL_GROUP_DIM] = h_p.astype(BF16)

    mixed = jnp.dot(cat_buf[...], wout_ref[...], preferred_element_type=F32)
    x1 = x + gate1 * mixed
    for slab in range(N_SLAB):
        o_ref[0, slab] = x1[:, slab * LANES:(slab + 1) * LANES]


def _mixer(x, mod, norm1_g, w_in, qk_conv_w, qk_conv_b, b_igate, b_fgate, mlstm_norm_g,
           pool_w, pool_scale, w_out):
    bsz, s, d = x.shape
    T = TIME_TILE
    H = MLSTM_HEADS
    n_main = 4 * D_MLSTM + D_POOL
    w_main = w_in[:, :n_main].astype(BF16)
    w_i = w_in[:, n_main:n_main + H]
    w_f = w_in[:, n_main + H:]
    wg = jnp.concatenate([jnp.pad(w_i, ((0, 0), (0, LANES - H))),
                          jnp.pad(w_f, ((0, 0), (0, LANES - H)))], axis=1).astype(BF16)
    bg = jnp.concatenate([jnp.pad(b_igate, (0, LANES - H)),
                          jnp.pad(b_fgate, (0, LANES - H))]).reshape(1, 2 * LANES)

    def const(shape):
        return pl.BlockSpec(shape, lambda b, j: (0,) * len(shape), pipeline_mode=pl.Buffered(1))

    return pl.pallas_call(
        _mixer_kernel,
        grid=(bsz, s // T),
        in_specs=[
            pl.BlockSpec((1, T, d), lambda b, j: (b, j, 0)),
            pl.BlockSpec((1, N_MOD, d), lambda b, j: (b, 0, 0)),
            const((1, d)),
            const((d, n_main)),
            const((d, 2 * LANES)),
            const((1, 2 * LANES)),
            const((QK_CONV_WIDTH, 2 * D_MLSTM)),
            const((1, 2 * D_MLSTM)),
            const((1, D_MLSTM)),
            const((len(POOL_WINDOWS), POOL_GROUP_DIM, POOL_GROUP_DIM)),
            const((1, D_POOL)),
            const((D_MLSTM + D_POOL, d)),
        ],
        out_specs=pl.BlockSpec((1, N_SLAB, T, LANES), lambda b, j: (b, 0, j, 0)),
        out_shape=jax.ShapeDtypeStruct((bsz, N_SLAB, s, LANES), F32),
        scratch_shapes=[
            pltpu.VMEM((N_SLAB, T, LANES), F32),
            pltpu.VMEM((3 * MLSTM_HEADS, T, LANES), F32),
            pltpu.VMEM((5, T, LANES), F32),
            pltpu.VMEM((MLSTM_HEADS, T, LANES), F32),
            pltpu.VMEM((T, D_MLSTM + D_POOL), BF16),
            pltpu.VMEM(((QK_CONV_WIDTH - 1) * SUBLANES, 2 * D_MLSTM), F32),
            pltpu.VMEM(((max(POOL_WINDOWS) - 1) * SUBLANES, D_POOL), F32),
            pltpu.VMEM((MLSTM_HEADS, HEAD_DIM, 2 * HEAD_DIM), F32),
            pltpu.VMEM((SUBLANES, LANES), F32),
        ],
        compiler_params=pltpu.CompilerParams(
            dimension_semantics=("arbitrary", "arbitrary"),
            vmem_limit_bytes=VMEM_LIMIT_BYTES),
        name="token_mixer",
    )(x, mod, norm1_g.reshape(1, d), w_main, wg, bg, qk_conv_w,
      qk_conv_b.reshape(1, -1), mlstm_norm_g.reshape(1, -1), pool_w.astype(BF16),
      pool_scale.reshape(1, -1), w_out.astype(BF16))


def _ffn_kernel(x_ref, mod_ref, g2_ref, wup_ref, cw_ref, cb_ref, wdn_ref, gf_ref, o_ref,
                act_buf, out_buf, tail_buf):
    T = TIME_TILE
    FB = FF_BLOCK
    G = SUBLANES
    j = pl.program_id(1)

    @pl.when(j == 0)
    def _():
        tail_buf[...] = jnp.zeros_like(tail_buf)

    x = jnp.concatenate([x_ref[0, k] for k in range(N_SLAB)], axis=1)
    shift2 = mod_ref[0, 3:4, :]
    scale2 = mod_ref[0, 4:5, :]
    gate2 = mod_ref[0, 5:6, :]
    hff = (_rms(x) * g2_ref[...] * (1.0 + scale2) + shift2).astype(BF16)
    n_back = FFN_CONV_WIDTH - 1

    def conv_block(col):
        cs = slice(col, col + FB)
        hist = _history(jnp.dot(hff, wup_ref[:, cs], preferred_element_type=F32), tail_buf, cs, n_back)
        out = cb_ref[:, cs]
        for kk in range(FFN_CONV_WIDTH):
            out = out + cw_ref[kk:kk + 1, cs] * hist[kk * G:kk * G + T]
        return out

    for blk in range(D_FF // FB):
        g = conv_block(blk * FB)
        val = conv_block(D_FF + blk * FB)
        act_buf[:, blk * FB:(blk + 1) * FB] = (g * jax.nn.sigmoid(g) * val).astype(BF16)

    y = jnp.dot(act_buf[...], wdn_ref[...], preferred_element_type=F32)
    res = _rms(x + gate2 * y) * gf_ref[...]
    for slab in range(N_SLAB):
        out_buf[slab] = res[:, slab * LANES:(slab + 1) * LANES]
    for slab in range(N_SLAB):
        for s in range(G):
            o_ref[0, s * STRAND:(s + 1) * STRAND, slab * LANES:(slab + 1) * LANES] = (
                out_buf[slab, _strand_rows(s), :])


def _ffn(x1, mod, norm2_g, w_up, ffn_conv_w, ffn_conv_b, w_down, final_norm_g):
    bsz, n_slab, s, _ = x1.shape
    d = D_MODEL
    T = TIME_TILE

    def const(shape):
        return pl.BlockSpec(shape, lambda b, j: (0,) * len(shape), pipeline_mode=pl.Buffered(1))

    return pl.pallas_call(
        _ffn_kernel,
        grid=(bsz, s // T),
        in_specs=[
            pl.BlockSpec((1, n_slab, T, LANES), lambda b, j: (b, 0, j, 0)),
            pl.BlockSpec((1, N_MOD, d), lambda b, j: (b, 0, 0)),
            const((1, d)),
            const((d, 2 * D_FF)),
            const((FFN_CONV_WIDTH, 2 * D_FF)),
            const((1, 2 * D_FF)),
            const((D_FF, d)),
            const((1, d)),
        ],
        out_specs=pl.BlockSpec((1, T, d), lambda b, j: (b, j, 0)),
        out_shape=jax.ShapeDtypeStruct((bsz, s, d), F32),
        scratch_shapes=[
            pltpu.VMEM((T, D_FF), BF16),
            pltpu.VMEM((n_slab, T, LANES), F32),
            pltpu.VMEM(((FFN_CONV_WIDTH - 1) * SUBLANES, 2 * D_FF), F32),
        ],
        compiler_params=pltpu.CompilerParams(
            dimension_semantics=("arbitrary", "arbitrary"),
            vmem_limit_bytes=VMEM_LIMIT_BYTES),
        name="channel_mixer",
    )(x1, mod, norm2_g.reshape(1, d), w_up.astype(BF16), ffn_conv_w,
      ffn_conv_b.reshape(1, -1), w_down.astype(BF16), final_norm_g.reshape(1, d))


def kernel(x, c, w_ada, b_ada, norm1_g, w_in, qk_conv_w, qk_conv_b, b_igate, b_fgate,
           mlstm_norm_g, pool_w, pool_scale, w_out, norm2_g, w_up, ffn_conv_w, ffn_conv_b,
           w_down, final_norm_g):
    bsz = x.shape[0]
    mod = _ada_mod(c, w_ada[0], b_ada[0]).reshape(bsz, N_MOD, D_MODEL)
    x1 = _mixer(x, mod, norm1_g[0], w_in[0], qk_conv_w[0], qk_conv_b[0], b_igate[0], b_fgate[0],
                mlstm_norm_g[0], pool_w[0], pool_scale[0], w_out[0])
    return _ffn(x1, mod, norm2_g[0], w_up[0], ffn_conv_w[0], ffn_conv_b[0], w_down[0], final_norm_g)
```

```python
import jax
import jax.numpy as jnp
from jax import lax
from jax.experimental import pallas as pl
from jax.experimental.pallas import tpu as pltpu

D_MODEL = 1024
MLSTM_HEADS = 4
D_MLSTM = 512
HEAD_DIM = 128
QK_CONV_WIDTH = 4
POOL_WINDOWS = (2, 4, 8, 16)
D_POOL = 512
POOL_GROUP_DIM = 128
D_FF = 2816
FFN_CONV_WIDTH = 3
N_MOD = 6
EPS = 1e-6

SUBLANES = 8
LANES = 128
VMEM_LIMIT_BYTES = 56 * 1024 * 1024

TIME_TILE = 512
STRAND = TIME_TILE // SUBLANES
CHUNK_STRANDS = 4
MLSTM_CHUNK = CHUNK_STRANDS * STRAND
FF_BLOCK = 256
N_SLAB = D_MODEL // LANES

F32 = jnp.float32
BF16 = jnp.bfloat16


def _rms(x):
    return x * lax.rsqrt(jnp.mean(x * x, axis=-1, keepdims=True) + EPS)


def _log_sigmoid(x):
    return jnp.minimum(x, 0.0) - jnp.log1p(jnp.exp(-jnp.abs(x)))


def _strand_rows(s):
    return pl.ds(s, STRAND, stride=SUBLANES)


def _tile_groups(group, n):
    return jnp.concatenate([group] * n, axis=0)


def _history(y, tail_ref, cols, n):
    T, W = y.shape
    G = SUBLANES
    last_sublane = lax.broadcasted_iota(jnp.int32, (G, W), 0) == G - 1
    groups = []
    for m in range(n, 0, -1):
        cur = y[T - G * m:T - G * (m - 1)]
        prev = tail_ref[(n - m) * G:(n - m + 1) * G, cols]
        groups.append(pltpu.roll(jnp.where(last_sublane, prev, cur), 1, 0))
    tail_ref[:, cols] = y[T - G * n:T]
    return jnp.concatenate(groups + [y], axis=0)


def _ada_kernel(c_ref, w_ref, b_ref, o_ref):
    cond = c_ref[...]
    cond = cond * jax.nn.sigmoid(cond)
    o_ref[...] = jnp.dot(cond, w_ref[...], precision=lax.Precision.HIGHEST,
                         preferred_element_type=F32) + b_ref[...]


def _ada_mod(c, w_ada, b_ada):
    bsz = c.shape[0]
    n = w_ada.shape[1]
    blk = D_MODEL
    return pl.pallas_call(
        _ada_kernel,
        grid=(n // blk,),
        in_specs=[pl.BlockSpec((bsz, D_MODEL), lambda i: (0, 0)),
                  pl.BlockSpec((D_MODEL, blk), lambda i: (0, i)),
                  pl.BlockSpec((1, blk), lambda i: (0, i))],
        out_specs=pl.BlockSpec((bsz, blk), lambda i: (0, i)),
        out_shape=jax.ShapeDtypeStruct((bsz, n), F32),
        name="ada_mod",
    )(c, w_ada, b_ada.reshape(1, n))


SC_E1, SC_W_INTER, SC_E_NEGM, SC_W_STATE, SC_R = range(5)


def _mixer_kernel(x_ref, mod_ref, g1_ref, win_ref, wg_ref, bg_ref, cw_ref, cb_ref, gm_ref,
                  pw_ref, ps_ref, wout_ref, o_ref,
                  xs_buf, qkv_buf, sc_buf, h_buf, cat_buf, qk_tail, u_tail, c_state, m_state):
    T = TIME_TILE
    G = SUBLANES
    NG = T // G
    NC = T // MLSTM_CHUNK
    H = MLSTM_HEADS
    j = pl.program_id(1)

    @pl.when(j == 0)
    def _():
        qk_tail[...] = jnp.zeros_like(qk_tail)
        u_tail[...] = jnp.zeros_like(u_tail)
        c_state[...] = jnp.zeros_like(c_state)
        m_state[...] = jnp.zeros_like(m_state)

    for slab in range(N_SLAB):
        for s in range(G):
            xs_buf[slab, _strand_rows(s), :] = (
                x_ref[0, s * STRAND:(s + 1) * STRAND, slab * LANES:(slab + 1) * LANES])
    x = jnp.concatenate([xs_buf[k] for k in range(N_SLAB)], axis=1)

    shift1 = mod_ref[0, 0:1, :]
    scale1 = mod_ref[0, 1:2, :]
    gate1 = mod_ref[0, 2:3, :]
    hmix = (_rms(x) * g1_ref[...] * (1.0 + scale1) + shift1).astype(BF16)

    def proj(lo, hi):
        return jnp.dot(hmix, win_ref[:, lo:hi], preferred_element_type=F32)


    g_col = jnp.dot(hmix, wg_ref[...], preferred_element_type=F32) + bg_ref[...]
    u = proj(4 * D_MLSTM, 4 * D_MLSTM + D_POOL)
    i_col = g_col[:, 0:LANES]
    f_col = _log_sigmoid(g_col[:, LANES:])

    a, b = f_col, i_col
    d = 1
    while d < STRAND:
        a_prev = jnp.concatenate([jnp.zeros((G * d, LANES), F32), a[0:T - G * d]], axis=0)
        b_prev = jnp.concatenate([jnp.full((G * d, LANES), -jnp.inf, F32), b[0:T - G * d]], axis=0)
        b = jnp.maximum(b_prev + a, b)
        a = a_prev + a
        d *= 2
    sub = lax.broadcasted_iota(jnp.int32, (G, LANES), 0)
    lv = 1
    while lv < CHUNK_STRANDS:
        pos = sub % (2 * lv)
        a_carry = jnp.zeros((G, LANES), F32)
        b_carry = jnp.full((G, LANES), -jnp.inf, F32)
        for off in range(1, lv + 1):
            take = pos == lv - 1 + off
            a_carry = jnp.where(take, pltpu.roll(a[T - G:T], off, 0), a_carry)
            b_carry = jnp.where(take, pltpu.roll(b[T - G:T], off, 0), b_carry)
        b = jnp.maximum(_tile_groups(b_carry, NG) + a, b)
        a = a + _tile_groups(a_carry, NG)
        lv *= 2
    b_col = a
    a_col = b
    o_pre = proj(3 * D_MLSTM, 4 * D_MLSTM)

    row = lax.broadcasted_iota(jnp.int32, (T, 1), 0)
    t_glob = j * T + (row % G) * STRAND + row // G
    n_tail = max(POOL_WINDOWS) - 1
    for g, win in enumerate(POOL_WINDOWS):
        gs = slice(g * POOL_GROUP_DIM, (g + 1) * POOL_GROUP_DIM)
        u_g = u[:, gs]
        acc = _history(u_g, u_tail.at[(n_tail - (win - 1)) * G:n_tail * G, :], gs, win - 1)
        d = 1
        while d < win:
            acc = acc[G * d:] + acc[:acc.shape[0] - G * d]
            d *= 2
        count = jnp.minimum(t_glob + 1, win).astype(F32)
        pooled = acc / count - u_g
        h_p = jnp.dot(pooled.astype(BF16), pw_ref[g], preferred_element_type=F32) * ps_ref[:, gs]
        cat_buf[:, D_MLSTM + g * POOL_GROUP_DIM:D_MLSTM + (g + 1) * POOL_GROUP_DIM] = h_p.astype(BF16)
    o_gate = jax.nn.sigmoid(o_pre)

    qk_pre = proj(0, 2 * D_MLSTM)

    b_end = b_col[T - G:T]
    a_end = a_col[T - G:T]
    m_c = m_state[0:1, :]
    m_prev = jnp.zeros((G, LANES), F32)
    m_new = jnp.zeros((G, LANES), F32)
    b_last = jnp.zeros((G, LANES), F32)
    decays = []
    for c in range(NC):
        row = CHUNK_STRANDS * (c + 1) - 1
        b_sum = b_end[row:row + 1, :]
        m_n = jnp.maximum(m_c + b_sum, a_end[row:row + 1, :])
        decays.append(jnp.exp(b_sum + m_c - m_n))
        in_chunk = (sub // CHUNK_STRANDS) == c
        m_prev = jnp.where(in_chunk, m_c, m_prev)
        m_new = jnp.where(in_chunk, m_n, m_new)
        b_last = jnp.where(in_chunk, b_sum, b_last)
        m_c = m_n
    m_state[0:1, :] = m_c
    m_prev = _tile_groups(m_prev, NG)
    m_new = _tile_groups(m_new, NG)
    b_last = _tile_groups(b_last, NG)
    m_t = jnp.maximum(b_col + m_prev, a_col)
    sc_buf[SC_E1] = b_col - m_t
    sc_buf[SC_W_INTER] = jnp.exp(b_col + m_prev - m_t)
    sc_buf[SC_E_NEGM] = jnp.exp(-m_t)
    sc_buf[SC_W_STATE] = jnp.exp(b_last - b_col + i_col - m_new)
    sc_buf[SC_R] = i_col - b_col

    v_all = proj(2 * D_MLSTM, 3 * D_MLSTM)

    n_back = QK_CONV_WIDTH - 1
    qk_hist = _history(qk_pre, qk_tail, slice(None), n_back)
    conv = cb_ref[...]
    for kk in range(QK_CONV_WIDTH):
        conv = conv + cw_ref[kk:kk + 1, :] * qk_hist[kk * G:kk * G + T]
    qk = conv * jax.nn.sigmoid(conv)
    for h in range(H):
        hs = slice(h * HEAD_DIM, (h + 1) * HEAD_DIM)
        qkv_buf[h] = qk[:, hs]
        qkv_buf[H + h] = qk[:, D_MLSTM + h * HEAD_DIM:D_MLSTM + (h + 1) * HEAD_DIM] * (HEAD_DIM ** -0.5)
        qkv_buf[2 * H + h] = v_all[:, hs]

    t_idx = lax.broadcasted_iota(jnp.int32, (MLSTM_CHUNK, MLSTM_CHUNK), 0)
    s_idx = lax.broadcasted_iota(jnp.int32, (MLSTM_CHUNK, MLSTM_CHUNK), 1)
    causal = s_idx <= t_idx
    ones = jnp.ones((MLSTM_CHUNK, HEAD_DIM), F32)

    states = [c_state[h] for h in range(H)]
    for c in range(NC):
        rows = [_strand_rows(CHUNK_STRANDS * c + k) for k in range(CHUNK_STRANDS)]

        def gather(ref, k):
            return jnp.concatenate([ref[k, r, :] for r in rows], axis=0)

        e1 = gather(sc_buf, SC_E1)
        w_inter = gather(sc_buf, SC_W_INTER)
        e_negm = gather(sc_buf, SC_E_NEGM)
        w_state = gather(sc_buf, SC_W_STATE)
        r_t = gather(sc_buf, SC_R).T
        for h in range(H):
            q_c = gather(qkv_buf, h).astype(BF16)
            k_c = gather(qkv_buf, H + h).astype(BF16)
            v_aug = jnp.concatenate([gather(qkv_buf, 2 * H + h), ones], axis=1)
            p = e1[:, h:h + 1] + r_t[h:h + 1, :]
            dm = jnp.exp(jnp.where(causal, p, -jnp.inf))
            scores = lax.dot_general(q_c, k_c, (((1,), (1,)), ((), ())),
                                     preferred_element_type=F32) * dm
            res = (jnp.dot(scores.astype(BF16), v_aug.astype(BF16), preferred_element_type=F32)
                   + w_inter[:, h:h + 1] * jnp.dot(q_c, states[h].astype(BF16),
                                                   preferred_element_type=F32))
            den = jnp.maximum(jnp.abs(res[:, HEAD_DIM:]), e_negm[:, h:h + 1])
            h_c = res[:, 0:HEAD_DIM] / den
            for k, r in enumerate(rows):
                h_buf[h, r, :] = h_c[k * STRAND:(k + 1) * STRAND]
            states[h] = decays[c][:, h:h + 1] * states[h] + lax.dot_general(
                k_c, (w_state[:, h:h + 1] * v_aug).astype(BF16), (((0,), (0,)), ((), ())),
                preferred_element_type=F32)
    for h in range(H):
        c_state[h] = states[h]

    h_all = jnp.concatenate([h_buf[h] for h in range(H)], axis=1)
    h_m = _rms(h_all) * gm_ref[...] * o_gate
    cat_buf[:, 0:D_MLSTM] = h_m.astype(BF16)

    mixed = jnp.dot(cat_buf[...], wout_ref[...], preferred_element_type=F32)
    x1 = x + gate1 * mixed
    for slab in range(N_SLAB):
        o_ref[0, slab] = x1[:, slab * LANES:(slab + 1) * LANES]


def _mixer(x, mod, norm1_g, w_in, qk_conv_w, qk_conv_b, b_igate, b_fgate, mlstm_norm_g,
           pool_w, pool_scale, w_out):
    bsz, s, d = x.shape
    T = TIME_TILE
    H = MLSTM_HEADS
    n_main = 4 * D_MLSTM + D_POOL
    w_main = w_in[:, :n_main].astype(BF16)
    w_i = w_in[:, n_main:n_main + H]
    w_f = w_in[:, n_main + H:]
    wg = jnp.concatenate([jnp.pad(w_i, ((0, 0), (0, LANES - H))),
                          jnp.pad(w_f, ((0, 0), (0, LANES - H)))], axis=1).astype(BF16)
    bg = jnp.concatenate([jnp.pad(b_igate, (0, LANES - H)),
                          jnp.pad(b_fgate, (0, LANES - H))]).reshape(1, 2 * LANES)

    def const(shape):
        return pl.BlockSpec(shape, lambda b, j: (0,) * len(shape), pipeline_mode=pl.Buffered(1))

    return pl.pallas_call(
        _mixer_kernel,
        grid=(bsz, s // T),
        in_specs=[
            pl.BlockSpec((1, T, d), lambda b, j: (b, j, 0)),
            pl.BlockSpec((1, N_MOD, d), lambda b, j: (b, 0, 0)),
            const((1, d)),
            const((d, n_main)),
            const((d, 2 * LANES)),
            const((1, 2 * LANES)),
            const((QK_CONV_WIDTH, 2 * D_MLSTM)),
            const((1, 2 * D_MLSTM)),
            const((1, D_MLSTM)),
            const((len(POOL_WINDOWS), POOL_GROUP_DIM, POOL_GROUP_DIM)),
            const((1, D_POOL)),
            const((D_MLSTM + D_POOL, d)),
        ],
        out_specs=pl.BlockSpec((1, N_SLAB, T, LANES), lambda b, j: (b, 0, j, 0)),
        out_shape=jax.ShapeDtypeStruct((bsz, N_SLAB, s, LANES), F32),
        scratch_shapes=[
            pltpu.VMEM((N_SLAB, T, LANES), F32),
            pltpu.VMEM((3 * MLSTM_HEADS, T, LANES), F32),
            pltpu.VMEM((5, T, LANES), F32),
            pltpu.VMEM((MLSTM_HEADS, T, LANES), F32),
            pltpu.VMEM((T, D_MLSTM + D_POOL), BF16),
            pltpu.VMEM(((QK_CONV_WIDTH - 1) * SUBLANES, 2 * D_MLSTM), F32),
            pltpu.VMEM(((max(POOL_WINDOWS) - 1) * SUBLANES, D_POOL), F32),
            pltpu.VMEM((MLSTM_HEADS, HEAD_DIM, 2 * HEAD_DIM), F32),
            pltpu.VMEM((SUBLANES, LANES), F32),
        ],
        compiler_params=pltpu.CompilerParams(
            dimension_semantics=("arbitrary", "arbitrary"),
            vmem_limit_bytes=VMEM_LIMIT_BYTES),
        name="token_mixer",
    )(x, mod, norm1_g.reshape(1, d), w_main, wg, bg, qk_conv_w,
      qk_conv_b.reshape(1, -1), mlstm_norm_g.reshape(1, -1), pool_w.astype(BF16),
      pool_scale.reshape(1, -1), w_out.astype(BF16))


def _ffn_kernel(x_ref, mod_ref, g2_ref, wup_ref, cw_ref, cb_ref, wdn_ref, gf_ref, o_ref,
                act_buf, out_buf, tail_buf):
    T = TIME_TILE
    FB = FF_BLOCK
    G = SUBLANES
    j = pl.program_id(1)

    @pl.when(j == 0)
    def _():
        tail_buf[...] = jnp.zeros_like(tail_buf)

    x = jnp.concatenate([x_ref[0, k] for k in range(N_SLAB)], axis=1)
    shift2 = mod_ref[0, 3:4, :]
    scale2 = mod_ref[0, 4:5, :]
    gate2 = mod_ref[0, 5:6, :]
    hff = (_rms(x) * g2_ref[...] * (1.0 + scale2) + shift2).astype(BF16)
    n_back = FFN_CONV_WIDTH - 1

    def conv_block(col):
        cs = slice(col, col + FB)
        hist = _history(jnp.dot(hff, wup_ref[:, cs], preferred_element_type=F32), tail_buf, cs, n_back)
        out = cb_ref[:, cs]
        for kk in range(FFN_CONV_WIDTH):
            out = out + cw_ref[kk:kk + 1, cs] * hist[kk * G:kk * G + T]
        return out

    for blk in range(D_FF // FB):
        g = conv_block(blk * FB)
        val = conv_block(D_FF + blk * FB)
        act_buf[:, blk * FB:(blk + 1) * FB] = (g * jax.nn.sigmoid(g) * val).astype(BF16)

    y = jnp.dot(act_buf[...], wdn_ref[...], preferred_element_type=F32)
    res = _rms(x + gate2 * y) * gf_ref[...]
    for slab in range(N_SLAB):
        out_buf[slab] = res[:, slab * LANES:(slab + 1) * LANES]
    for slab in range(N_SLAB):
        for s in range(G):
            o_ref[0, s * STRAND:(s + 1) * STRAND, slab * LANES:(slab + 1) * LANES] = (
                out_buf[slab, _strand_rows(s), :])


def _ffn(x1, mod, norm2_g, w_up, ffn_conv_w, ffn_conv_b, w_down, final_norm_g):
    bsz, n_slab, s, _ = x1.shape
    d = D_MODEL
    T = TIME_TILE

    def const(shape):
        return pl.BlockSpec(shape, lambda b, j: (0,) * len(shape), pipeline_mode=pl.Buffered(1))

    return pl.pallas_call(
        _ffn_kernel,
        grid=(bsz, s // T),
        in_specs=[
            pl.BlockSpec((1, n_slab, T, LANES), lambda b, j: (b, 0, j, 0)),
            pl.BlockSpec((1, N_MOD, d), lambda b, j: (b, 0, 0)),
            const((1, d)),
            const((d, 2 * D_FF)),
            const((FFN_CONV_WIDTH, 2 * D_FF)),
            const((1, 2 * D_FF)),
            const((D_FF, d)),
            const((1, d)),
        ],
        out_specs=pl.BlockSpec((1, T, d), lambda b, j: (b, j, 0)),
        out_shape=jax.ShapeDtypeStruct((bsz, s, d), F32),
        scratch_shapes=[
            pltpu.VMEM((T, D_FF), BF16),
            pltpu.VMEM((n_slab, T, LANES), F32),
            pltpu.VMEM(((FFN_CONV_WIDTH - 1) * SUBLANES, 2 * D_FF), F32),
        ],
        compiler_params=pltpu.CompilerParams(
            dimension_semantics=("arbitrary", "arbitrary"),
            vmem_limit_bytes=VMEM_LIMIT_BYTES),
        name="channel_mixer",
    )(x1, mod, norm2_g.reshape(1, d), w_up.astype(BF16), ffn_conv_w,
      ffn_conv_b.reshape(1, -1), w_down.astype(BF16), final_norm_g.reshape(1, d))


def kernel(x, c, w_ada, b_ada, norm1_g, w_in, qk_conv_w, qk_conv_b, b_igate, b_fgate,
           mlstm_norm_g, pool_w, pool_scale, w_out, norm2_g, w_up, ffn_conv_w, ffn_conv_b,
           w_down, final_norm_g):
    bsz = x.shape[0]
    mod = _ada_mod(c, w_ada[0], b_ada[0]).reshape(bsz, N_MOD, D_MODEL)
    x1 = _mixer(x, mod, norm1_g[0], w_in[0], qk_conv_w[0], qk_conv_b[0], b_igate[0], b_fgate[0],
                mlstm_norm_g[0], pool_w[0], pool_scale[0], w_out[0])
    return _ffn(x1, mod, norm2_g[0], w_up[0], ffn_conv_w[0], ffn_conv_b[0], w_down[0], final_norm_g)
```

```python
import jax
import jax.numpy as jnp
from jax import lax
from jax.experimental import pallas as pl
from jax.experimental.pallas import tpu as pltpu

D_MODEL = 1024
MLSTM_HEADS = 4
D_MLSTM = 512
HEAD_DIM = 128
QK_CONV_WIDTH = 4
POOL_WINDOWS = (2, 4, 8, 16)
D_POOL = 512
POOL_GROUP_DIM = 128
D_FF = 2816
FFN_CONV_WIDTH = 3
N_MOD = 6
EPS = 1e-6

SUBLANES = 8
LANES = 128
VMEM_LIMIT_BYTES = 56 * 1024 * 1024

TIME_TILE = 512
STRAND = TIME_TILE // SUBLANES
CHUNK_STRANDS = 4
MLSTM_CHUNK = CHUNK_STRANDS * STRAND
FF_BLOCK = 256
N_SLAB = D_MODEL // LANES

F32 = jnp.float32
BF16 = jnp.bfloat16


def _rms(x):
    return x * lax.rsqrt(jnp.mean(x * x, axis=-1, keepdims=True) + EPS)


def _log_sigmoid(x):
    return jnp.minimum(x, 0.0) - jnp.log1p(jnp.exp(-jnp.abs(x)))


def _strand_rows(s):
    return pl.ds(s, STRAND, stride=SUBLANES)


def _strand_copies(hbm_ref, vmem_ref, sem, tile, slot, tiles_per_seq, to_vmem):
    b = tile // tiles_per_seq
    t0 = (tile % tiles_per_seq) * TIME_TILE
    copies = []
    for s in range(SUBLANES):
        hbm = hbm_ref.at[b, pl.ds(t0 + s * STRAND, STRAND), :]
        vmem = vmem_ref.at[slot, :, s, :]
        src, dst = (hbm, vmem) if to_vmem else (vmem, hbm)
        copies.append(pltpu.make_async_copy(src, dst, sem.at[slot]))
    return copies


def _tile_groups(group, n):
    return jnp.concatenate([group] * n, axis=0)


def _history(y, tail_ref, cols, n):
    T, W = y.shape
    G = SUBLANES
    last_sublane = lax.broadcasted_iota(jnp.int32, (G, W), 0) == G - 1
    groups = []
    for m in range(n, 0, -1):
        cur = y[T - G * m:T - G * (m - 1)]
        prev = tail_ref[(n - m) * G:(n - m + 1) * G, cols]
        groups.append(pltpu.roll(jnp.where(last_sublane, prev, cur), 1, 0))
    tail_ref[:, cols] = y[T - G * n:T]
    return jnp.concatenate(groups + [y], axis=0)


def _ada_kernel(c_ref, w_ref, b_ref, o_ref):
    cond = c_ref[...]
    cond = cond * jax.nn.sigmoid(cond)
    o_ref[...] = jnp.dot(cond, w_ref[...], precision=lax.Precision.HIGHEST,
                         preferred_element_type=F32) + b_ref[...]


def _ada_mod(c, w_ada, b_ada):
    bsz = c.shape[0]
    n = w_ada.shape[1]
    blk = D_MODEL
    return pl.pallas_call(
        _ada_kernel,
        grid=(n // blk,),
        in_specs=[pl.BlockSpec((bsz, D_MODEL), lambda i: (0, 0)),
                  pl.BlockSpec((D_MODEL, blk), lambda i: (0, i)),
                  pl.BlockSpec((1, blk), lambda i: (0, i))],
        out_specs=pl.BlockSpec((bsz, blk), lambda i: (0, i)),
        out_shape=jax.ShapeDtypeStruct((bsz, n), F32),
        name="ada_mod",
    )(c, w_ada, b_ada.reshape(1, n))


SC_E1, SC_W_INTER, SC_E_NEGM, SC_W_STATE, SC_R = range(5)


def _mixer_kernel(x_hbm, mod_ref, g1_ref, win_ref, wg_ref, bg_ref, cw_ref, cb_ref, gm_ref,
                  pw_ref, ps_ref, wout_ref, o_ref,
                  x_buf, x_sem, qkv_buf, sc_buf, h_buf, cat_buf, qk_tail, u_tail, c_state, m_state):
    T = TIME_TILE
    G = SUBLANES
    NG = T // G
    NC = T // MLSTM_CHUNK
    H = MLSTM_HEADS
    j = pl.program_id(1)
    n_seq = pl.num_programs(1)
    tile = pl.program_id(0) * n_seq + j
    n_tiles = pl.num_programs(0) * n_seq
    slot = tile % 2

    @pl.when(tile == 0)
    def _():
        for cp in _strand_copies(x_hbm, x_buf, x_sem, tile, slot, n_seq, True):
            cp.start()

    @pl.when(tile + 1 < n_tiles)
    def _():
        for cp in _strand_copies(x_hbm, x_buf, x_sem, tile + 1, 1 - slot, n_seq, True):
            cp.start()

    for cp in _strand_copies(x_hbm, x_buf, x_sem, tile, slot, n_seq, True):
        cp.wait()

    @pl.when(j == 0)
    def _():
        qk_tail[...] = jnp.zeros_like(qk_tail)
        u_tail[...] = jnp.zeros_like(u_tail)
        c_state[...] = jnp.zeros_like(c_state)
        m_state[...] = jnp.zeros_like(m_state)

    x = x_buf[slot].reshape(T, D_MODEL)

    shift1 = mod_ref[0, 0:1, :]
    scale1 = mod_ref[0, 1:2, :]
    gate1 = mod_ref[0, 2:3, :]
    hmix = (_rms(x) * g1_ref[...] * (1.0 + scale1) + shift1).astype(BF16)

    def proj(lo, hi):
        return jnp.dot(hmix, win_ref[:, lo:hi], preferred_element_type=F32)


    g_col = jnp.dot(hmix, wg_ref[...], preferred_element_type=F32) + bg_ref[...]
    u = proj(4 * D_MLSTM, 4 * D_MLSTM + D_POOL)
    i_col = g_col[:, 0:LANES]
    f_col = _log_sigmoid(g_col[:, LANES:])

    a, b = f_col, i_col
    d = 1
    while d < STRAND:
        a_prev = jnp.concatenate([jnp.zeros((G * d, LANES), F32), a[0:T - G * d]], axis=0)
        b_prev = jnp.concatenate([jnp.full((G * d, LANES), -jnp.inf, F32), b[0:T - G * d]], axis=0)
        b = jnp.maximum(b_prev + a, b)
        a = a_prev + a
        d *= 2
    sub = lax.broadcasted_iota(jnp.int32, (G, LANES), 0)
    lv = 1
    while lv < CHUNK_STRANDS:
        pos = sub % (2 * lv)
        a_carry = jnp.zeros((G, LANES), F32)
        b_carry = jnp.full((G, LANES), -jnp.inf, F32)
        for off in range(1, lv + 1):
            take = pos == lv - 1 + off
            a_carry = jnp.where(take, pltpu.roll(a[T - G:T], off, 0), a_carry)
            b_carry = jnp.where(take, pltpu.roll(b[T - G:T], off, 0), b_carry)
        b = jnp.maximum(_tile_groups(b_carry, NG) + a, b)
        a = a + _tile_groups(a_carry, NG)
        lv *= 2
    b_col = a
    a_col = b
    o_pre = proj(3 * D_MLSTM, 4 * D_MLSTM)

    row = lax.broadcasted_iota(jnp.int32, (T, 1), 0)
    t_glob = j * T + (row % G) * STRAND + row // G
    n_tail = max(POOL_WINDOWS) - 1
    for g, win in enumerate(POOL_WINDOWS):
        gs = slice(g * POOL_GROUP_DIM, (g + 1) * POOL_GROUP_DIM)
        u_g = u[:, gs]
        acc = _history(u_g, u_tail.at[(n_tail - (win - 1)) * G:n_tail * G, :], gs, win - 1)
        d = 1
        while d < win:
            acc = acc[G * d:] + acc[:acc.shape[0] - G * d]
            d *= 2
        count = jnp.minimum(t_glob + 1, win).astype(F32)
        pooled = acc / count - u_g
        h_p = jnp.dot(pooled.astype(BF16), pw_ref[g], preferred_element_type=F32) * ps_ref[:, gs]
        cat_buf[:, D_MLSTM + g * POOL_GROUP_DIM:D_MLSTM + (g + 1) * POOL_GROUP_DIM] = h_p.astype(BF16)
    o_gate = jax.nn.sigmoid(o_pre)

    qk_pre = proj(0, 2 * D_MLSTM)

    b_end = b_col[T - G:T]
    a_end = a_col[T - G:T]
    m_c = m_state[0:1, :]
    m_prev = jnp.zeros((G, LANES), F32)
    m_new = jnp.zeros((G, LANES), F32)
    b_last = jnp.zeros((G, LANES), F32)
    decays = []
    for c in range(NC):
        end_row = CHUNK_STRANDS * (c + 1) - 1
        b_sum = b_end[end_row:end_row + 1, :]
        m_n = jnp.maximum(m_c + b_sum, a_end[end_row:end_row + 1, :])
        decays.append(jnp.exp(b_sum + m_c - m_n))
        in_chunk = (sub // CHUNK_STRANDS) == c
        m_prev = jnp.where(in_chunk, m_c, m_prev)
        m_new = jnp.where(in_chunk, m_n, m_new)
        b_last = jnp.where(in_chunk, b_sum, b_last)
        m_c = m_n
    m_state[0:1, :] = m_c
    m_prev = _tile_groups(m_prev, NG)
    m_new = _tile_groups(m_new, NG)
    b_last = _tile_groups(b_last, NG)
    m_t = jnp.maximum(b_col + m_prev, a_col)
    sc_buf[SC_E1] = b_col - m_t
    sc_buf[SC_W_INTER] = jnp.exp(b_col + m_prev - m_t)
    sc_buf[SC_E_NEGM] = jnp.exp(-m_t)
    sc_buf[SC_W_STATE] = jnp.exp(b_last - b_col + i_col - m_new)
    sc_buf[SC_R] = i_col - b_col

    v_all = proj(2 * D_MLSTM, 3 * D_MLSTM)

    n_back = QK_CONV_WIDTH - 1
    qk_hist = _history(qk_pre, qk_tail, slice(None), n_back)
    conv = cb_ref[...]
    for kk in range(QK_CONV_WIDTH):
        conv = conv + cw_ref[kk:kk + 1, :] * qk_hist[kk * G:kk * G + T]
    qk = conv * jax.nn.sigmoid(conv)
    for h in range(H):
        hs = slice(h * HEAD_DIM, (h + 1) * HEAD_DIM)
        qkv_buf[h] = qk[:, hs]
        qkv_buf[H + h] = qk[:, D_MLSTM + h * HEAD_DIM:D_MLSTM + (h + 1) * HEAD_DIM] * (HEAD_DIM ** -0.5)
        qkv_buf[2 * H + h] = v_all[:, hs]

    t_idx = lax.broadcasted_iota(jnp.int32, (MLSTM_CHUNK, MLSTM_CHUNK), 0)
    s_idx = lax.broadcasted_iota(jnp.int32, (MLSTM_CHUNK, MLSTM_CHUNK), 1)
    causal = s_idx <= t_idx
    ones = jnp.ones((MLSTM_CHUNK, HEAD_DIM), F32)

    states = [c_state[h] for h in range(H)]
    for c in range(NC):
        rows = [_strand_rows(CHUNK_STRANDS * c + k) for k in range(CHUNK_STRANDS)]

        def gather(ref, k):
            return jnp.concatenate([ref[k, r, :] for r in rows], axis=0)

        e1 = gather(sc_buf, SC_E1)
        w_inter = gather(sc_buf, SC_W_INTER)
        e_negm = gather(sc_buf, SC_E_NEGM)
        w_state = gather(sc_buf, SC_W_STATE)
        r_t = gather(sc_buf, SC_R).T
        for h in range(H):
            q_c = gather(qkv_buf, h).astype(BF16)
            k_c = gather(qkv_buf, H + h).astype(BF16)
            v_aug = jnp.concatenate([gather(qkv_buf, 2 * H + h), ones], axis=1)
            p = e1[:, h:h + 1] + r_t[h:h + 1, :]
            dm = jnp.exp(jnp.where(causal, p, -jnp.inf))
            scores = lax.dot_general(q_c, k_c, (((1,), (1,)), ((), ())),
                                     preferred_element_type=F32) * dm
            res = (jnp.dot(scores.astype(BF16), v_aug.astype(BF16), preferred_element_type=F32)
                   + w_inter[:, h:h + 1] * jnp.dot(q_c, states[h].astype(BF16),
                                                   preferred_element_type=F32))
            den = jnp.maximum(jnp.abs(res[:, HEAD_DIM:]), e_negm[:, h:h + 1])
            h_c = res[:, 0:HEAD_DIM] / den
            for k, r in enumerate(rows):
                h_buf[h, r, :] = h_c[k * STRAND:(k + 1) * STRAND]
            states[h] = decays[c][:, h:h + 1] * states[h] + lax.dot_general(
                k_c, (w_state[:, h:h + 1] * v_aug).astype(BF16), (((0,), (0,)), ((), ())),
                preferred_element_type=F32)
    for h in range(H):
        c_state[h] = states[h]

    h_all = jnp.concatenate([h_buf[h] for h in range(H)], axis=1)
    h_m = _rms(h_all) * gm_ref[...] * o_gate
    cat_buf[:, 0:D_MLSTM] = h_m.astype(BF16)

    mixed = jnp.dot(cat_buf[...], wout_ref[...], preferred_element_type=F32)
    x1 = x_buf[slot].reshape(T, D_MODEL) + gate1 * mixed
    for slab in range(N_SLAB):
        o_ref[0, slab] = x1[:, slab * LANES:(slab + 1) * LANES]


def _mixer(x, mod, norm1_g, w_in, qk_conv_w, qk_conv_b, b_igate, b_fgate, mlstm_norm_g,
           pool_w, pool_scale, w_out):
    bsz, s, d = x.shape
    T = TIME_TILE
    H = MLSTM_HEADS
    n_main = 4 * D_MLSTM + D_POOL
    w_main = w_in[:, :n_main].astype(BF16)
    w_i = w_in[:, n_main:n_main + H]
    w_f = w_in[:, n_main + H:]
    wg = jnp.concatenate([jnp.pad(w_i, ((0, 0), (0, LANES - H))),
                          jnp.pad(w_f, ((0, 0), (0, LANES - H)))], axis=1).astype(BF16)
    bg = jnp.concatenate([jnp.pad(b_igate, (0, LANES - H)),
                          jnp.pad(b_fgate, (0, LANES - H))]).reshape(1, 2 * LANES)

    def const(shape):
        return pl.BlockSpec(shape, lambda b, j: (0,) * len(shape), pipeline_mode=pl.Buffered(1))

    return pl.pallas_call(
        _mixer_kernel,
        grid=(bsz, s // T),
        in_specs=[
            pl.BlockSpec(memory_space=pl.ANY),
            pl.BlockSpec((1, N_MOD, d), lambda b, j: (b, 0, 0)),
            const((1, d)),
            const((d, n_main)),
            const((d, 2 * LANES)),
            const((1, 2 * LANES)),
            const((QK_CONV_WIDTH, 2 * D_MLSTM)),
            const((1, 2 * D_MLSTM)),
            const((1, D_MLSTM)),
            const((len(POOL_WINDOWS), POOL_GROUP_DIM, POOL_GROUP_DIM)),
            const((1, D_POOL)),
            const((D_MLSTM + D_POOL, d)),
        ],
        out_specs=pl.BlockSpec((1, N_SLAB, T, LANES), lambda b, j: (b, 0, j, 0)),
        out_shape=jax.ShapeDtypeStruct((bsz, N_SLAB, s, LANES), F32),
        scratch_shapes=[
            pltpu.VMEM((2, STRAND, SUBLANES, d), F32),
            pltpu.SemaphoreType.DMA((2,)),
            pltpu.VMEM((3 * MLSTM_HEADS, T, LANES), F32),
            pltpu.VMEM((5, T, LANES), F32),
            pltpu.VMEM((MLSTM_HEADS, T, LANES), F32),
            pltpu.VMEM((T, D_MLSTM + D_POOL), BF16),
            pltpu.VMEM(((QK_CONV_WIDTH - 1) * SUBLANES, 2 * D_MLSTM), F32),
            pltpu.VMEM(((max(POOL_WINDOWS) - 1) * SUBLANES, D_POOL), F32),
            pltpu.VMEM((MLSTM_HEADS, HEAD_DIM, 2 * HEAD_DIM), F32),
            pltpu.VMEM((SUBLANES, LANES), F32),
        ],
        compiler_params=pltpu.CompilerParams(
            dimension_semantics=("arbitrary", "arbitrary"),
            vmem_limit_bytes=VMEM_LIMIT_BYTES),
        name="token_mixer",
    )(x, mod, norm1_g.reshape(1, d), w_main, wg, bg, qk_conv_w,
      qk_conv_b.reshape(1, -1), mlstm_norm_g.reshape(1, -1), pool_w.astype(BF16),
      pool_scale.reshape(1, -1), w_out.astype(BF16))


def _ffn_kernel(x_ref, mod_ref, g2_ref, wup_ref, cw_ref, cb_ref, wdn_ref, gf_ref, o_hbm,
                act_buf, out_buf, out_sem, tail_buf):
    T = TIME_TILE
    FB = FF_BLOCK
    G = SUBLANES
    j = pl.program_id(1)
    n_seq = pl.num_programs(1)
    tile = pl.program_id(0) * n_seq + j
    n_tiles = pl.num_programs(0) * n_seq
    slot = tile % 2

    @pl.when(tile >= 2)
    def _():
        for cp in _strand_copies(o_hbm, out_buf, out_sem, tile - 2, slot, n_seq, False):
            cp.wait()

    @pl.when(j == 0)
    def _():
        tail_buf[...] = jnp.zeros_like(tail_buf)

    x = jnp.concatenate([x_ref[0, k] for k in range(N_SLAB)], axis=1)
    shift2 = mod_ref[0, 3:4, :]
    scale2 = mod_ref[0, 4:5, :]
    gate2 = mod_ref[0, 5:6, :]
    hff = (_rms(x) * g2_ref[...] * (1.0 + scale2) + shift2).astype(BF16)
    n_back = FFN_CONV_WIDTH - 1

    def conv_block(col):
        cs = slice(col, col + FB)
        hist = _history(jnp.dot(hff, wup_ref[:, cs], preferred_element_type=F32), tail_buf, cs, n_back)
        out = cb_ref[:, cs]
        for kk in range(FFN_CONV_WIDTH):
            out = out + cw_ref[kk:kk + 1, cs] * hist[kk * G:kk * G + T]
        return out

    for blk in range(D_FF // FB):
        g = conv_block(blk * FB)
        val = conv_block(D_FF + blk * FB)
        act_buf[:, blk * FB:(blk + 1) * FB] = (g * jax.nn.sigmoid(g) * val).astype(BF16)

    y = jnp.dot(act_buf[...], wdn_ref[...], preferred_element_type=F32)
    res = _rms(x + gate2 * y) * gf_ref[...]
    out_buf[slot] = res.reshape(STRAND, G, D_MODEL)
    for cp in _strand_copies(o_hbm, out_buf, out_sem, tile, slot, n_seq, False):
        cp.start()

    @pl.when(tile == n_tiles - 1)
    def _():
        for cp in _strand_copies(o_hbm, out_buf, out_sem, tile - 1, 1 - slot, n_seq, False):
            cp.wait()
        for cp in _strand_copies(o_hbm, out_buf, out_sem, tile, slot, n_seq, False):
            cp.wait()


def _ffn(x1, mod, norm2_g, w_up, ffn_conv_w, ffn_conv_b, w_down, final_norm_g):
    bsz, n_slab, s, _ = x1.shape
    d = D_MODEL
    T = TIME_TILE

    def const(shape):
        return pl.BlockSpec(shape, lambda b, j: (0,) * len(shape), pipeline_mode=pl.Buffered(1))

    return pl.pallas_call(
        _ffn_kernel,
        grid=(bsz, s // T),
        in_specs=[
            pl.BlockSpec((1, n_slab, T, LANES), lambda b, j: (b, 0, j, 0)),
            pl.BlockSpec((1, N_MOD, d), lambda b, j: (b, 0, 0)),
            const((1, d)),
            const((d, 2 * D_FF)),
            const((FFN_CONV_WIDTH, 2 * D_FF)),
            const((1, 2 * D_FF)),
            const((D_FF, d)),
            const((1, d)),
        ],
        out_specs=pl.BlockSpec(memory_space=pl.ANY),
        out_shape=jax.ShapeDtypeStruct((bsz, s, d), F32),
        scratch_shapes=[
            pltpu.VMEM((T, D_FF), BF16),
            pltpu.VMEM((2, STRAND, SUBLANES, d), F32),
            pltpu.SemaphoreType.DMA((2,)),
            pltpu.VMEM(((FFN_CONV_WIDTH - 1) * SUBLANES, 2 * D_FF), F32),
        ],
        compiler_params=pltpu.CompilerParams(
            dimension_semantics=("arbitrary", "arbitrary"),
            vmem_limit_bytes=VMEM_LIMIT_BYTES),
        name="channel_mixer",
    )(x1, mod, norm2_g.reshape(1, d), w_up.astype(BF16), ffn_conv_w,
      ffn_conv_b.reshape(1, -1), w_down.astype(BF16), final_norm_g.reshape(1, d))


def kernel(x, c, w_ada, b_ada, norm1_g, w_in, qk_conv_w, qk_conv_b, b_igate, b_fgate,
           mlstm_norm_g, pool_w, pool_scale, w_out, norm2_g, w_up, ffn_conv_w, ffn_conv_b,
           w_down, final_norm_g):
    bsz = x.shape[0]
    mod = _ada_mod(c, w_ada[0], b_ada[0]).reshape(bsz, N_MOD, D_MODEL)
    x1 = _mixer(x, mod, norm1_g[0], w_in[0], qk_conv_w[0], qk_conv_b[0], b_igate[0], b_fgate[0],
                mlstm_norm_g[0], pool_w[0], pool_scale[0], w_out[0])
    return _ffn(x1, mod, norm2_g[0], w_up[0], ffn_conv_w[0], ffn_conv_b[0], w_down[0], final_norm_g)
```

```python
import jax
import jax.numpy as jnp
from jax import lax
from jax.experimental import pallas as pl
from jax.experimental.pallas import tpu as pltpu

D_MODEL = 1024
MLSTM_HEADS = 4
D_MLSTM = 512
HEAD_DIM = 128
QK_CONV_WIDTH = 4
POOL_WINDOWS = (2, 4, 8, 16)
D_POOL = 512
POOL_GROUP_DIM = 128
D_FF = 2816
FFN_CONV_WIDTH = 3
N_MOD = 6
EPS = 1e-6

SUBLANES = 8
LANES = 128
VMEM_LIMIT_BYTES = 56 * 1024 * 1024

TIME_TILE = 1024
STRAND = TIME_TILE // SUBLANES
CHUNK_STRANDS = 2
MLSTM_CHUNK = CHUNK_STRANDS * STRAND
FF_BLOCK = 256
N_SLAB = D_MODEL // LANES

F32 = jnp.float32
BF16 = jnp.bfloat16


def _rms(x):
    return x * lax.rsqrt(jnp.mean(x * x, axis=-1, keepdims=True) + EPS)


def _log_sigmoid(x):
    return jnp.minimum(x, 0.0) - jnp.log1p(jnp.exp(-jnp.abs(x)))


def _strand_rows(s):
    return pl.ds(s, STRAND, stride=SUBLANES)


def _strand_copies(hbm_ref, vmem_ref, sem, tile, slot, tiles_per_seq, to_vmem):
    b = tile // tiles_per_seq
    t0 = (tile % tiles_per_seq) * TIME_TILE
    copies = []
    for s in range(SUBLANES):
        hbm = hbm_ref.at[b, pl.ds(t0 + s * STRAND, STRAND), :]
        vmem = vmem_ref.at[slot, :, s, :]
        src, dst = (hbm, vmem) if to_vmem else (vmem, hbm)
        copies.append(pltpu.make_async_copy(src, dst, sem.at[slot]))
    return copies


def _tile_groups(group, n):
    return jnp.concatenate([group] * n, axis=0)


def _history(y, tail_ref, cols, n):
    T, W = y.shape
    G = SUBLANES
    last_sublane = lax.broadcasted_iota(jnp.int32, (G, W), 0) == G - 1
    groups = []
    for m in range(n, 0, -1):
        cur = y[T - G * m:T - G * (m - 1)]
        prev = tail_ref[(n - m) * G:(n - m + 1) * G, cols]
        groups.append(pltpu.roll(jnp.where(last_sublane, prev, cur), 1, 0))
    tail_ref[:, cols] = y[T - G * n:T]
    return jnp.concatenate(groups + [y], axis=0)


def _ada_kernel(c_ref, w_ref, b_ref, o_ref):
    cond = c_ref[...]
    cond = cond * jax.nn.sigmoid(cond)
    o_ref[...] = jnp.dot(cond.astype(BF16), w_ref[...].astype(BF16),
                         preferred_element_type=F32) + b_ref[...]


def _ada_mod(c, w_ada, b_ada):
    bsz = c.shape[0]
    n = w_ada.shape[1]
    blk = D_MODEL
    return pl.pallas_call(
        _ada_kernel,
        grid=(n // blk,),
        in_specs=[pl.BlockSpec((bsz, D_MODEL), lambda i: (0, 0)),
                  pl.BlockSpec((D_MODEL, blk), lambda i: (0, i)),
                  pl.BlockSpec((1, blk), lambda i: (0, i))],
        out_specs=pl.BlockSpec((bsz, blk), lambda i: (0, i)),
        out_shape=jax.ShapeDtypeStruct((bsz, n), F32),
        name="ada_mod",
    )(c, w_ada, b_ada.reshape(1, n))


SC_E1, SC_W_INTER, SC_E_NEGM, SC_W_STATE, SC_R = range(5)


def _mixer_kernel(x_hbm, mod_ref, g1_ref, win_ref, wg_ref, bg_ref, cw_ref, cb_ref, gm_ref,
                  pw_ref, ps_ref, wout_ref, o_ref,
                  x_buf, x_sem, qkv_buf, sc_buf, h_buf, cat_buf, qk_tail, u_tail, c_state, m_state):
    T = TIME_TILE
    G = SUBLANES
    NG = T // G
    NC = T // MLSTM_CHUNK
    H = MLSTM_HEADS
    j = pl.program_id(1)
    n_seq = pl.num_programs(1)
    tile = pl.program_id(0) * n_seq + j
    n_tiles = pl.num_programs(0) * n_seq
    slot = tile % 2

    @pl.when(tile == 0)
    def _():
        for cp in _strand_copies(x_hbm, x_buf, x_sem, tile, slot, n_seq, True):
            cp.start()

    @pl.when(tile + 1 < n_tiles)
    def _():
        for cp in _strand_copies(x_hbm, x_buf, x_sem, tile + 1, 1 - slot, n_seq, True):
            cp.start()

    for cp in _strand_copies(x_hbm, x_buf, x_sem, tile, slot, n_seq, True):
        cp.wait()

    @pl.when(j == 0)
    def _():
        qk_tail[...] = jnp.zeros_like(qk_tail)
        u_tail[...] = jnp.zeros_like(u_tail)
        c_state[...] = jnp.zeros_like(c_state)
        m_state[...] = jnp.zeros_like(m_state)

    x = x_buf[slot].reshape(T, D_MODEL)

    shift1 = mod_ref[0, 0:1, :]
    scale1 = mod_ref[0, 1:2, :]
    gate1 = mod_ref[0, 2:3, :]
    hmix = (_rms(x) * g1_ref[...] * (1.0 + scale1) + shift1).astype(BF16)

    def proj(lo, hi):
        return jnp.dot(hmix, win_ref[:, lo:hi], preferred_element_type=F32)


    g_col = jnp.dot(hmix, wg_ref[...], preferred_element_type=F32) + bg_ref[...]
    u = proj(4 * D_MLSTM, 4 * D_MLSTM + D_POOL)
    i_col = g_col[:, 0:LANES]
    f_col = _log_sigmoid(g_col[:, LANES:])

    a, b = f_col, i_col
    d = 1
    while d < STRAND:
        a_prev = jnp.concatenate([jnp.zeros((G * d, LANES), F32), a[0:T - G * d]], axis=0)
        b_prev = jnp.concatenate([jnp.full((G * d, LANES), -jnp.inf, F32), b[0:T - G * d]], axis=0)
        b = jnp.maximum(b_prev + a, b)
        a = a_prev + a
        d *= 2
    sub = lax.broadcasted_iota(jnp.int32, (G, LANES), 0)
    lv = 1
    while lv < CHUNK_STRANDS:
        pos = sub % (2 * lv)
        a_carry = jnp.zeros((G, LANES), F32)
        b_carry = jnp.full((G, LANES), -jnp.inf, F32)
        for off in range(1, lv + 1):
            take = pos == lv - 1 + off
            a_carry = jnp.where(take, pltpu.roll(a[T - G:T], off, 0), a_carry)
            b_carry = jnp.where(take, pltpu.roll(b[T - G:T], off, 0), b_carry)
        b = jnp.maximum(_tile_groups(b_carry, NG) + a, b)
        a = a + _tile_groups(a_carry, NG)
        lv *= 2
    b_col = a
    a_col = b
    o_pre = proj(3 * D_MLSTM, 4 * D_MLSTM)

    row = lax.broadcasted_iota(jnp.int32, (T, 1), 0)
    t_glob = j * T + (row % G) * STRAND + row // G
    n_tail = max(POOL_WINDOWS) - 1
    for g, win in enumerate(POOL_WINDOWS):
        gs = slice(g * POOL_GROUP_DIM, (g + 1) * POOL_GROUP_DIM)
        u_g = u[:, gs]
        acc = _history(u_g, u_tail.at[(n_tail - (win - 1)) * G:n_tail * G, :], gs, win - 1)
        d = 1
        while d < win:
            acc = acc[G * d:] + acc[:acc.shape[0] - G * d]
            d *= 2
        count = jnp.minimum(t_glob + 1, win).astype(F32)
        pooled = acc / count - u_g
        h_p = jnp.dot(pooled.astype(BF16), pw_ref[g], preferred_element_type=F32) * ps_ref[:, gs]
        cat_buf[:, D_MLSTM + g * POOL_GROUP_DIM:D_MLSTM + (g + 1) * POOL_GROUP_DIM] = h_p.astype(BF16)
    o_gate = jax.nn.sigmoid(o_pre)

    qk_pre = proj(0, 2 * D_MLSTM)

    b_end = b_col[T - G:T]
    a_end = a_col[T - G:T]
    m_c = m_state[0:1, :]
    m_prev = jnp.zeros((G, LANES), F32)
    m_new = jnp.zeros((G, LANES), F32)
    b_last = jnp.zeros((G, LANES), F32)
    decays = []
    for c in range(NC):
        end_row = CHUNK_STRANDS * (c + 1) - 1
        b_sum = b_end[end_row:end_row + 1, :]
        m_n = jnp.maximum(m_c + b_sum, a_end[end_row:end_row + 1, :])
        decays.append(jnp.exp(b_sum + m_c - m_n))
        in_chunk = (sub // CHUNK_STRANDS) == c
        m_prev = jnp.where(in_chunk, m_c, m_prev)
        m_new = jnp.where(in_chunk, m_n, m_new)
        b_last = jnp.where(in_chunk, b_sum, b_last)
        m_c = m_n
    m_state[0:1, :] = m_c
    m_prev = _tile_groups(m_prev, NG)
    m_new = _tile_groups(m_new, NG)
    b_last = _tile_groups(b_last, NG)
    m_t = jnp.maximum(b_col + m_prev, a_col)
    sc_buf[SC_E1] = b_col - m_t
    sc_buf[SC_W_INTER] = jnp.exp(b_col + m_prev - m_t)
    sc_buf[SC_E_NEGM] = jnp.exp(-m_t)
    sc_buf[SC_W_STATE] = jnp.exp(b_last - b_col + i_col - m_new)
    sc_buf[SC_R] = i_col - b_col

    v_all = proj(2 * D_MLSTM, 3 * D_MLSTM)

    n_back = QK_CONV_WIDTH - 1
    qk_hist = _history(qk_pre, qk_tail, slice(None), n_back)
    conv = cb_ref[...]
    for kk in range(QK_CONV_WIDTH):
        conv = conv + cw_ref[kk:kk + 1, :] * qk_hist[kk * G:kk * G + T]
    qk = conv * jax.nn.sigmoid(conv)
    for h in range(H):
        hs = slice(h * HEAD_DIM, (h + 1) * HEAD_DIM)
        qkv_buf[h] = qk[:, hs]
        qkv_buf[H + h] = qk[:, D_MLSTM + h * HEAD_DIM:D_MLSTM + (h + 1) * HEAD_DIM] * (HEAD_DIM ** -0.5)
        qkv_buf[2 * H + h] = v_all[:, hs]

    t_idx = lax.broadcasted_iota(jnp.int32, (MLSTM_CHUNK, MLSTM_CHUNK), 0)
    s_idx = lax.broadcasted_iota(jnp.int32, (MLSTM_CHUNK, MLSTM_CHUNK), 1)
    causal = s_idx <= t_idx
    ones = jnp.ones((MLSTM_CHUNK, HEAD_DIM), F32)

    states = [c_state[h] for h in range(H)]
    for c in range(NC):
        rows = [_strand_rows(CHUNK_STRANDS * c + k) for k in range(CHUNK_STRANDS)]

        def gather(ref, k):
            return jnp.concatenate([ref[k, r, :] for r in rows], axis=0)

        e1 = gather(sc_buf, SC_E1)
        w_inter = gather(sc_buf, SC_W_INTER)
        e_negm = gather(sc_buf, SC_E_NEGM)
        w_state = gather(sc_buf, SC_W_STATE)
        r_t = gather(sc_buf, SC_R).T
        for h in range(H):
            q_c = gather(qkv_buf, h).astype(BF16)
            k_c = gather(qkv_buf, H + h).astype(BF16)
            v_aug = jnp.concatenate([gather(qkv_buf, 2 * H + h), ones], axis=1)
            p = e1[:, h:h + 1] + r_t[h:h + 1, :]
            dm = jnp.exp(jnp.where(causal, p, -jnp.inf))
            scores = lax.dot_general(q_c, k_c, (((1,), (1,)), ((), ())),
                                     preferred_element_type=F32) * dm
            res = (jnp.dot(scores.astype(BF16), v_aug.astype(BF16), preferred_element_type=F32)
                   + w_inter[:, h:h + 1] * jnp.dot(q_c, states[h].astype(BF16),
                                                   preferred_element_type=F32))
            den = jnp.maximum(jnp.abs(res[:, HEAD_DIM:]), e_negm[:, h:h + 1])
            h_c = res[:, 0:HEAD_DIM] / den
            for k, r in enumerate(rows):
                h_buf[h, r, :] = h_c[k * STRAND:(k + 1) * STRAND]
            states[h] = decays[c][:, h:h + 1] * states[h] + lax.dot_general(
                k_c, (w_state[:, h:h + 1] * v_aug).astype(BF16), (((0,), (0,)), ((), ())),
                preferred_element_type=F32)
    for h in range(H):
        c_state[h] = states[h]

    h_all = jnp.concatenate([h_buf[h] for h in range(H)], axis=1)
    h_m = _rms(h_all) * gm_ref[...] * o_gate
    cat_buf[:, 0:D_MLSTM] = h_m.astype(BF16)

    mixed = jnp.dot(cat_buf[...], wout_ref[...], preferred_element_type=F32)
    x1 = x_buf[slot].reshape(T, D_MODEL) + gate1 * mixed
    for slab in range(N_SLAB):
        o_ref[0, slab] = x1[:, slab * LANES:(slab + 1) * LANES]


def _mixer(x, mod, norm1_g, w_in, qk_conv_w, qk_conv_b, b_igate, b_fgate, mlstm_norm_g,
           pool_w, pool_scale, w_out):
    bsz, s, d = x.shape
    T = TIME_TILE
    H = MLSTM_HEADS
    n_main = 4 * D_MLSTM + D_POOL
    w_main = w_in[:, :n_main].astype(BF16)
    w_i = w_in[:, n_main:n_main + H]
    w_f = w_in[:, n_main + H:]
    wg = jnp.concatenate([jnp.pad(w_i, ((0, 0), (0, LANES - H))),
                          jnp.pad(w_f, ((0, 0), (0, LANES - H)))], axis=1).astype(BF16)
    bg = jnp.concatenate([jnp.pad(b_igate, (0, LANES - H)),
                          jnp.pad(b_fgate, (0, LANES - H))]).reshape(1, 2 * LANES)

    def const(shape):
        return pl.BlockSpec(shape, lambda b, j: (0,) * len(shape), pipeline_mode=pl.Buffered(1))

    return pl.pallas_call(
        _mixer_kernel,
        grid=(bsz, s // T),
        in_specs=[
            pl.BlockSpec(memory_space=pl.ANY),
            pl.BlockSpec((1, N_MOD, d), lambda b, j: (b, 0, 0)),
            const((1, d)),
            const((d, n_main)),
            const((d, 2 * LANES)),
            const((1, 2 * LANES)),
            const((QK_CONV_WIDTH, 2 * D_MLSTM)),
            const((1, 2 * D_MLSTM)),
            const((1, D_MLSTM)),
            const((len(POOL_WINDOWS), POOL_GROUP_DIM, POOL_GROUP_DIM)),
            const((1, D_POOL)),
            const((D_MLSTM + D_POOL, d)),
        ],
        out_specs=pl.BlockSpec((1, N_SLAB, T, LANES), lambda b, j: (b, 0, j, 0)),
        out_shape=jax.ShapeDtypeStruct((bsz, N_SLAB, s, LANES), F32),
        scratch_shapes=[
            pltpu.VMEM((2, STRAND, SUBLANES, d), F32),
            pltpu.SemaphoreType.DMA((2,)),
            pltpu.VMEM((3 * MLSTM_HEADS, T, LANES), F32),
            pltpu.VMEM((5, T, LANES), F32),
            pltpu.VMEM((MLSTM_HEADS, T, LANES), F32),
            pltpu.VMEM((T, D_MLSTM + D_POOL), BF16),
            pltpu.VMEM(((QK_CONV_WIDTH - 1) * SUBLANES, 2 * D_MLSTM), F32),
            pltpu.VMEM(((max(POOL_WINDOWS) - 1) * SUBLANES, D_POOL), F32),
            pltpu.VMEM((MLSTM_HEADS, HEAD_DIM, 2 * HEAD_DIM), F32),
            pltpu.VMEM((SUBLANES, LANES), F32),
        ],
        compiler_params=pltpu.CompilerParams(
            dimension_semantics=("arbitrary", "arbitrary"),
            vmem_limit_bytes=VMEM_LIMIT_BYTES),
        name="token_mixer",
    )(x, mod, norm1_g.reshape(1, d), w_main, wg, bg, qk_conv_w,
      qk_conv_b.reshape(1, -1), mlstm_norm_g.reshape(1, -1), pool_w.astype(BF16),
      pool_scale.reshape(1, -1), w_out.astype(BF16))


def _ffn_kernel(x_ref, mod_ref, g2_ref, wup_ref, cw_ref, cb_ref, wdn_ref, gf_ref, o_hbm,
                act_buf, out_buf, out_sem, tail_buf):
    T = TIME_TILE
    FB = FF_BLOCK
    G = SUBLANES
    j = pl.program_id(1)
    n_seq = pl.num_programs(1)
    tile = pl.program_id(0) * n_seq + j
    n_tiles = pl.num_programs(0) * n_seq
    slot = tile % 2

    @pl.when(tile >= 2)
    def _():
        for cp in _strand_copies(o_hbm, out_buf, out_sem, tile - 2, slot, n_seq, False):
            cp.wait()

    @pl.when(j == 0)
    def _():
        tail_buf[...] = jnp.zeros_like(tail_buf)

    x = jnp.concatenate([x_ref[0, k] for k in range(N_SLAB)], axis=1)
    shift2 = mod_ref[0, 3:4, :]
    scale2 = mod_ref[0, 4:5, :]
    gate2 = mod_ref[0, 5:6, :]
    hff = (_rms(x) * g2_ref[...] * (1.0 + scale2) + shift2).astype(BF16)
    n_back = FFN_CONV_WIDTH - 1

    def conv_block(col):
        cs = slice(col, col + FB)
        hist = _history(jnp.dot(hff, wup_ref[:, cs], preferred_element_type=F32), tail_buf, cs, n_back)
        out = cb_ref[:, cs]
        for kk in range(FFN_CONV_WIDTH):
            out = out + cw_ref[kk:kk + 1, cs] * hist[kk * G:kk * G + T]
        return out

    for blk in range(D_FF // FB):
        g = conv_block(blk * FB)
        val = conv_block(D_FF + blk * FB)
        act_buf[:, blk * FB:(blk + 1) * FB] = (g * jax.nn.sigmoid(g) * val).astype(BF16)

    y = jnp.dot(act_buf[...], wdn_ref[...], preferred_element_type=F32)
    res = _rms(x + gate2 * y) * gf_ref[...]
    out_buf[slot] = res.reshape(STRAND, G, D_MODEL)
    for cp in _strand_copies(o_hbm, out_buf, out_sem, tile, slot, n_seq, False):
        cp.start()

    @pl.when(tile == n_tiles - 1)
    def _():
        for cp in _strand_copies(o_hbm, out_buf, out_sem, tile - 1, 1 - slot, n_seq, False):
            cp.wait()
        for cp in _strand_copies(o_hbm, out_buf, out_sem, tile, slot, n_seq, False):
            cp.wait()


def _ffn(x1, mod, norm2_g, w_up, ffn_conv_w, ffn_conv_b, w_down, final_norm_g):
    bsz, n_slab, s, _ = x1.shape
    d = D_MODEL
    T = TIME_TILE

    def const(shape):
        return pl.BlockSpec(shape, lambda b, j: (0,) * len(shape), pipeline_mode=pl.Buffered(1))

    return pl.pallas_call(
        _ffn_kernel,
        grid=(bsz, s // T),
        in_specs=[
            pl.BlockSpec((1, n_slab, T, LANES), lambda b, j: (b, 0, j, 0)),
            pl.BlockSpec((1, N_MOD, d), lambda b, j: (b, 0, 0)),
            const((1, d)),
            const((d, 2 * D_FF)),
            const((FFN_CONV_WIDTH, 2 * D_FF)),
            const((1, 2 * D_FF)),
            const((D_FF, d)),
            const((1, d)),
        ],
        out_specs=pl.BlockSpec(memory_space=pl.ANY),
        out_shape=jax.ShapeDtypeStruct((bsz, s, d), F32),
        scratch_shapes=[
            pltpu.VMEM((T, D_FF), BF16),
            pltpu.VMEM((2, STRAND, SUBLANES, d), F32),
            pltpu.SemaphoreType.DMA((2,)),
            pltpu.VMEM(((FFN_CONV_WIDTH - 1) * SUBLANES, 2 * D_FF), F32),
        ],
        compiler_params=pltpu.CompilerParams(
            dimension_semantics=("arbitrary", "arbitrary"),
            vmem_limit_bytes=VMEM_LIMIT_BYTES),
        name="channel_mixer",
    )(x1, mod, norm2_g.reshape(1, d), w_up.astype(BF16), ffn_conv_w,
      ffn_conv_b.reshape(1, -1), w_down.astype(BF16), final_norm_g.reshape(1, d))


def kernel(x, c, w_ada, b_ada, norm1_g, w_in, qk_conv_w, qk_conv_b, b_igate, b_fgate,
           mlstm_norm_g, pool_w, pool_scale, w_out, norm2_g, w_up, ffn_conv_w, ffn_conv_b,
           w_down, final_norm_g):
    bsz = x.shape[0]
    mod = _ada_mod(c, w_ada[0], b_ada[0]).reshape(bsz, N_MOD, D_MODEL)
    x1 = _mixer(x, mod, norm1_g[0], w_in[0], qk_conv_w[0], qk_conv_b[0], b_igate[0], b_fgate[0],
                mlstm_norm_g[0], pool_w[0], pool_scale[0], w_out[0])
    return _ffn(x1, mod, norm2_g[0], w_up[0], ffn_conv_w[0], ffn_conv_b[0], w_down[0], final_norm_g)
```

```python
import jax
import jax.numpy as jnp
from jax import lax
from jax.experimental import pallas as pl
from jax.experimental.pallas import tpu as pltpu

D_MODEL = 1024
MLSTM_HEADS = 4
D_MLSTM = 512
HEAD_DIM = 128
QK_CONV_WIDTH = 4
POOL_WINDOWS = (2, 4, 8, 16)
D_POOL = 512
POOL_GROUP_DIM = 128
D_FF = 2816
FFN_CONV_WIDTH = 3
N_MOD = 6
EPS = 1e-6

SUBLANES = 8
LANES = 128
VMEM_LIMIT_BYTES = 56 * 1024 * 1024

TIME_TILE = 1024
STRAND = TIME_TILE // SUBLANES
CHUNK_STRANDS = 2
MLSTM_CHUNK = CHUNK_STRANDS * STRAND
FF_BLOCK = 256
SCAN_BLOCK = 8
ROW_PIECE = 256
N_SLAB = D_MODEL // LANES

F32 = jnp.float32
BF16 = jnp.bfloat16


def _rms(x):
    return x * lax.rsqrt(jnp.mean(x * x, axis=-1, keepdims=True) + EPS)


def _log_sigmoid(x):
    return jnp.minimum(x, 0.0) - jnp.log1p(jnp.exp(-jnp.abs(x)))


def _strand_rows(s):
    return pl.ds(s, STRAND, stride=SUBLANES)


def _strand_copies(hbm_ref, vmem_ref, sem, tile, slot, tiles_per_seq, to_vmem):
    b = tile // tiles_per_seq
    t0 = (tile % tiles_per_seq) * TIME_TILE
    copies = []
    for s in range(SUBLANES):
        hbm = hbm_ref.at[b, pl.ds(t0 + s * STRAND, STRAND), :]
        vmem = vmem_ref.at[slot, :, s, :]
        src, dst = (hbm, vmem) if to_vmem else (vmem, hbm)
        copies.append(pltpu.make_async_copy(src, dst, sem.at[slot]))
    return copies


def _tile_groups(group, n):
    return jnp.concatenate([group] * n, axis=0)


def _history(y, tail_ref, cols, n):
    T, W = y.shape
    G = SUBLANES
    last_sublane = lax.broadcasted_iota(jnp.int32, (G, W), 0) == G - 1
    groups = []
    for m in range(n, 0, -1):
        cur = y[T - G * m:T - G * (m - 1)]
        prev = tail_ref[(n - m) * G:(n - m + 1) * G, cols]
        groups.append(pltpu.roll(jnp.where(last_sublane, prev, cur), 1, 0))
    tail_ref[:, cols] = y[T - G * n:T]
    return jnp.concatenate(groups + [y], axis=0)


def _ada_kernel(c_ref, w_ref, b_ref, o_ref):
    cond = c_ref[...]
    cond = cond * jax.nn.sigmoid(cond)
    o_ref[...] = jnp.dot(cond.astype(BF16), w_ref[...].astype(BF16),
                         preferred_element_type=F32) + b_ref[...]


def _ada_mod(c, w_ada, b_ada):
    bsz = c.shape[0]
    n = w_ada.shape[1]
    blk = D_MODEL
    return pl.pallas_call(
        _ada_kernel,
        grid=(n // blk,),
        in_specs=[pl.BlockSpec((bsz, D_MODEL), lambda i: (0, 0)),
                  pl.BlockSpec((D_MODEL, blk), lambda i: (0, i)),
                  pl.BlockSpec((1, blk), lambda i: (0, i))],
        out_specs=pl.BlockSpec((bsz, blk), lambda i: (0, i)),
        out_shape=jax.ShapeDtypeStruct((bsz, n), F32),
        name="ada_mod",
    )(c, w_ada, b_ada.reshape(1, n))


SC_E1, SC_W_INTER, SC_E_NEGM, SC_W_STATE, SC_R = range(5)


def _mixer_kernel(x_hbm, mod_ref, g1_ref, win_ref, wg_ref, bg_ref, cw_ref, cb_ref, gm_ref,
                  pw_ref, ps_ref, wout_ref, o_ref,
                  x_buf, x_sem, qkv_buf, sc_buf, h_buf, cat_buf, qk_tail, u_tail, c_state, m_state):
    T = TIME_TILE
    G = SUBLANES
    NG = T // G
    NC = T // MLSTM_CHUNK
    H = MLSTM_HEADS
    j = pl.program_id(1)
    n_seq = pl.num_programs(1)
    tile = pl.program_id(0) * n_seq + j
    n_tiles = pl.num_programs(0) * n_seq
    slot = tile % 2

    @pl.when(tile == 0)
    def _():
        for cp in _strand_copies(x_hbm, x_buf, x_sem, tile, slot, n_seq, True):
            cp.start()

    @pl.when(tile + 1 < n_tiles)
    def _():
        for cp in _strand_copies(x_hbm, x_buf, x_sem, tile + 1, 1 - slot, n_seq, True):
            cp.start()

    for cp in _strand_copies(x_hbm, x_buf, x_sem, tile, slot, n_seq, True):
        cp.wait()

    @pl.when(j == 0)
    def _():
        qk_tail[...] = jnp.zeros_like(qk_tail)
        u_tail[...] = jnp.zeros_like(u_tail)
        c_state[...] = jnp.zeros_like(c_state)
        m_state[...] = jnp.zeros_like(m_state)

    x = x_buf[slot].reshape(T, D_MODEL)

    shift1 = mod_ref[0, 0:1, :]
    scale1 = mod_ref[0, 1:2, :]
    gate1 = mod_ref[0, 2:3, :]
    hmix = (_rms(x) * (g1_ref[...] * (1.0 + scale1)) + shift1).astype(BF16)

    def proj(lo, hi):
        return jnp.dot(hmix, win_ref[:, lo:hi], preferred_element_type=F32)


    g_col = jnp.dot(hmix, wg_ref[...], preferred_element_type=F32) + bg_ref[...]
    u = proj(4 * D_MLSTM, 4 * D_MLSTM + D_POOL)
    i_col = g_col[:, 0:LANES]
    f_col = _log_sigmoid(g_col[:, LANES:])

    zero_g = jnp.zeros((G, LANES), F32)
    ninf_g = jnp.full((G, LANES), -jnp.inf, F32)
    rows_b = SCAN_BLOCK * G
    blocks = []
    for k in range(NG // SCAN_BLOCK):
        a = f_col[k * rows_b:(k + 1) * rows_b]
        b = i_col[k * rows_b:(k + 1) * rows_b]
        d = 1
        while d < SCAN_BLOCK:
            a_prev = jnp.concatenate([_tile_groups(zero_g, d), a[0:rows_b - G * d]], axis=0)
            b_prev = jnp.concatenate([_tile_groups(ninf_g, d), b[0:rows_b - G * d]], axis=0)
            b = jnp.maximum(b_prev + a, b)
            a = a_prev + a
            d *= 2
        blocks.append((a, b))
    tot_a, tot_b = zero_g, ninf_g
    before = []
    for a, b in blocks:
        before.append((tot_a, tot_b))
        tot_b = jnp.maximum(tot_b + a[rows_b - G:], b[rows_b - G:])
        tot_a = tot_a + a[rows_b - G:]
    sub = lax.broadcasted_iota(jnp.int32, (G, LANES), 0)
    left_a, left_b = zero_g, ninf_g
    lv = 1
    while lv < CHUNK_STRANDS:
        pos = sub % (2 * lv)
        a_carry, b_carry = zero_g, ninf_g
        for off in range(1, lv + 1):
            take = pos == lv - 1 + off
            a_carry = jnp.where(take, pltpu.roll(tot_a, off, 0), a_carry)
            b_carry = jnp.where(take, pltpu.roll(tot_b, off, 0), b_carry)
        left_b = jnp.maximum(b_carry + left_a, left_b)
        left_a = a_carry + left_a
        tot_b = jnp.maximum(b_carry + tot_a, tot_b)
        tot_a = a_carry + tot_a
        lv *= 2
    a_parts, b_parts = [], []
    for (a, b), (pre_a, pre_b) in zip(blocks, before):
        pre_b = jnp.maximum(left_b + pre_a, pre_b)
        pre_a = left_a + pre_a
        b_parts.append(jnp.maximum(_tile_groups(pre_b, SCAN_BLOCK) + a, b))
        a_parts.append(a + _tile_groups(pre_a, SCAN_BLOCK))
    b_col = jnp.concatenate(a_parts, axis=0)
    a_col = jnp.concatenate(b_parts, axis=0)
    o_pre = proj(3 * D_MLSTM, 4 * D_MLSTM)

    row = lax.broadcasted_iota(jnp.int32, (T, 1), 0)
    t_glob = j * T + (row % G) * STRAND + row // G
    n_tail = max(POOL_WINDOWS) - 1
    for g, win in enumerate(POOL_WINDOWS):
        gs = slice(g * POOL_GROUP_DIM, (g + 1) * POOL_GROUP_DIM)
        u_g = u[:, gs]
        acc = _history(u_g, u_tail.at[(n_tail - (win - 1)) * G:n_tail * G, :], gs, win - 1)
        d = 1
        while d < win:
            acc = acc[G * d:] + acc[:acc.shape[0] - G * d]
            d *= 2
        count = jnp.minimum(t_glob + 1, win).astype(F32)
        pooled = acc / count - u_g
        h_p = jnp.dot(pooled.astype(BF16), pw_ref[g], preferred_element_type=F32) * ps_ref[:, gs]
        cat_buf[:, D_MLSTM + g * POOL_GROUP_DIM:D_MLSTM + (g + 1) * POOL_GROUP_DIM] = h_p.astype(BF16)
    o_gate = jax.nn.sigmoid(o_pre)

    qk_pre = proj(0, 2 * D_MLSTM)

    b_end = b_col[T - G:T]
    a_end = a_col[T - G:T]
    m_c = m_state[0:1, :]
    m_prev = jnp.zeros((G, LANES), F32)
    m_new = jnp.zeros((G, LANES), F32)
    b_last = jnp.zeros((G, LANES), F32)
    decays = []
    for c in range(NC):
        end_row = CHUNK_STRANDS * (c + 1) - 1
        b_sum = b_end[end_row:end_row + 1, :]
        m_n = jnp.maximum(m_c + b_sum, a_end[end_row:end_row + 1, :])
        decays.append(jnp.exp(b_sum + m_c - m_n))
        in_chunk = (sub // CHUNK_STRANDS) == c
        m_prev = jnp.where(in_chunk, m_c, m_prev)
        m_new = jnp.where(in_chunk, m_n, m_new)
        b_last = jnp.where(in_chunk, b_sum, b_last)
        m_c = m_n
    m_state[0:1, :] = m_c
    m_prev = _tile_groups(m_prev, NG)
    m_new = _tile_groups(m_new, NG)
    b_last = _tile_groups(b_last, NG)
    m_t = jnp.maximum(b_col + m_prev, a_col)
    sc_buf[SC_E1] = b_col - m_t
    sc_buf[SC_W_INTER] = jnp.exp(b_col + m_prev - m_t)
    sc_buf[SC_E_NEGM] = jnp.exp(-m_t)
    sc_buf[SC_W_STATE] = jnp.exp(b_last - b_col + i_col - m_new)
    sc_buf[SC_R] = i_col - b_col

    v_all = proj(2 * D_MLSTM, 3 * D_MLSTM)

    n_back = QK_CONV_WIDTH - 1
    qk_hist = _history(qk_pre, qk_tail, slice(None), n_back)
    conv = cb_ref[...]
    for kk in range(QK_CONV_WIDTH):
        conv = conv + cw_ref[kk:kk + 1, :] * qk_hist[kk * G:kk * G + T]
    qk = conv * jax.nn.sigmoid(conv)
    for h in range(H):
        hs = slice(h * HEAD_DIM, (h + 1) * HEAD_DIM)
        qkv_buf[h] = qk[:, hs]
        qkv_buf[H + h] = qk[:, D_MLSTM + h * HEAD_DIM:D_MLSTM + (h + 1) * HEAD_DIM] * (HEAD_DIM ** -0.5)
        qkv_buf[2 * H + h] = v_all[:, hs]

    t_idx = lax.broadcasted_iota(jnp.int32, (MLSTM_CHUNK, MLSTM_CHUNK), 0)
    s_idx = lax.broadcasted_iota(jnp.int32, (MLSTM_CHUNK, MLSTM_CHUNK), 1)
    causal = s_idx <= t_idx
    ones = jnp.ones((MLSTM_CHUNK, HEAD_DIM), F32)

    states = [c_state[h] for h in range(H)]
    for c in range(NC):
        rows = [_strand_rows(CHUNK_STRANDS * c + k) for k in range(CHUNK_STRANDS)]

        def gather(ref, k):
            return jnp.concatenate([ref[k, r, :] for r in rows], axis=0)

        e1 = gather(sc_buf, SC_E1)
        w_inter = gather(sc_buf, SC_W_INTER)
        e_negm = gather(sc_buf, SC_E_NEGM)
        w_state = gather(sc_buf, SC_W_STATE)
        r_t = gather(sc_buf, SC_R).T
        for h in range(H):
            q_c = gather(qkv_buf, h).astype(BF16)
            k_c = gather(qkv_buf, H + h).astype(BF16)
            v_aug = jnp.concatenate([gather(qkv_buf, 2 * H + h), ones], axis=1)
            p = e1[:, h:h + 1] + r_t[h:h + 1, :]
            dm = jnp.exp(jnp.where(causal, p, -jnp.inf))
            scores = lax.dot_general(q_c, k_c, (((1,), (1,)), ((), ())),
                                     preferred_element_type=F32) * dm
            res = (jnp.dot(scores.astype(BF16), v_aug.astype(BF16), preferred_element_type=F32)
                   + w_inter[:, h:h + 1] * jnp.dot(q_c, states[h].astype(BF16),
                                                   preferred_element_type=F32))
            den = jnp.maximum(jnp.abs(res[:, HEAD_DIM:]), e_negm[:, h:h + 1])
            h_c = res[:, 0:HEAD_DIM] / den
            for k, r in enumerate(rows):
                h_buf[h, r, :] = h_c[k * STRAND:(k + 1) * STRAND]
            states[h] = decays[c][:, h:h + 1] * states[h] + lax.dot_general(
                k_c, (w_state[:, h:h + 1] * v_aug).astype(BF16), (((0,), (0,)), ((), ())),
                preferred_element_type=F32)
    for h in range(H):
        c_state[h] = states[h]

    h_all = jnp.concatenate([h_buf[h] for h in range(H)], axis=1)
    h_m = _rms(h_all) * gm_ref[...] * o_gate
    cat_buf[:, 0:D_MLSTM] = h_m.astype(BF16)

    mixed = jnp.dot(cat_buf[...], wout_ref[...], preferred_element_type=F32)
    x1 = x_buf[slot].reshape(T, D_MODEL) + gate1 * mixed
    for slab in range(N_SLAB):
        o_ref[0, slab] = x1[:, slab * LANES:(slab + 1) * LANES]


def _mixer(x, mod, norm1_g, w_in, qk_conv_w, qk_conv_b, b_igate, b_fgate, mlstm_norm_g,
           pool_w, pool_scale, w_out):
    bsz, s, d = x.shape
    T = TIME_TILE
    H = MLSTM_HEADS
    n_main = 4 * D_MLSTM + D_POOL
    w_main = w_in[:, :n_main].astype(BF16)
    w_i = w_in[:, n_main:n_main + H]
    w_f = w_in[:, n_main + H:]
    wg = jnp.concatenate([jnp.pad(w_i, ((0, 0), (0, LANES - H))),
                          jnp.pad(w_f, ((0, 0), (0, LANES - H)))], axis=1).astype(BF16)
    bg = jnp.concatenate([jnp.pad(b_igate, (0, LANES - H)),
                          jnp.pad(b_fgate, (0, LANES - H))]).reshape(1, 2 * LANES)

    def const(shape):
        return pl.BlockSpec(shape, lambda b, j: (0,) * len(shape), pipeline_mode=pl.Buffered(1))

    return pl.pallas_call(
        _mixer_kernel,
        grid=(bsz, s // T),
        in_specs=[
            pl.BlockSpec(memory_space=pl.ANY),
            pl.BlockSpec((1, N_MOD, d), lambda b, j: (b, 0, 0)),
            const((1, d)),
            const((d, n_main)),
            const((d, 2 * LANES)),
            const((1, 2 * LANES)),
            const((QK_CONV_WIDTH, 2 * D_MLSTM)),
            const((1, 2 * D_MLSTM)),
            const((1, D_MLSTM)),
            const((len(POOL_WINDOWS), POOL_GROUP_DIM, POOL_GROUP_DIM)),
            const((1, D_POOL)),
            const((D_MLSTM + D_POOL, d)),
        ],
        out_specs=pl.BlockSpec((1, N_SLAB, T, LANES), lambda b, j: (b, 0, j, 0)),
        out_shape=jax.ShapeDtypeStruct((bsz, N_SLAB, s, LANES), F32),
        scratch_shapes=[
            pltpu.VMEM((2, STRAND, SUBLANES, d), F32),
            pltpu.SemaphoreType.DMA((2,)),
            pltpu.VMEM((3 * MLSTM_HEADS, T, LANES), F32),
            pltpu.VMEM((5, T, LANES), F32),
            pltpu.VMEM((MLSTM_HEADS, T, LANES), F32),
            pltpu.VMEM((T, D_MLSTM + D_POOL), BF16),
            pltpu.VMEM(((QK_CONV_WIDTH - 1) * SUBLANES, 2 * D_MLSTM), F32),
            pltpu.VMEM(((max(POOL_WINDOWS) - 1) * SUBLANES, D_POOL), F32),
            pltpu.VMEM((MLSTM_HEADS, HEAD_DIM, 2 * HEAD_DIM), F32),
            pltpu.VMEM((SUBLANES, LANES), F32),
        ],
        compiler_params=pltpu.CompilerParams(
            dimension_semantics=("arbitrary", "arbitrary"),
            vmem_limit_bytes=VMEM_LIMIT_BYTES),
        name="token_mixer",
    )(x, mod, norm1_g.reshape(1, d), w_main, wg, bg, qk_conv_w,
      qk_conv_b.reshape(1, -1), mlstm_norm_g.reshape(1, -1), pool_w.astype(BF16),
      pool_scale.reshape(1, -1), w_out.astype(BF16))


def _ffn_kernel(x_ref, mod_ref, g2_ref, wup_ref, cw_ref, cb_ref, wdn_ref, gf_ref, o_hbm,
                act_buf, out_buf, out_sem, tail_buf):
    T = TIME_TILE
    FB = FF_BLOCK
    G = SUBLANES
    j = pl.program_id(1)
    n_seq = pl.num_programs(1)
    tile = pl.program_id(0) * n_seq + j
    n_tiles = pl.num_programs(0) * n_seq
    slot = tile % 2

    @pl.when(tile >= 2)
    def _():
        for cp in _strand_copies(o_hbm, out_buf, out_sem, tile - 2, slot, n_seq, False):
            cp.wait()

    @pl.when(j == 0)
    def _():
        tail_buf[...] = jnp.zeros_like(tail_buf)

    x = jnp.concatenate([x_ref[0, k] for k in range(N_SLAB)], axis=1)
    shift2 = mod_ref[0, 3:4, :]
    scale2 = mod_ref[0, 4:5, :]
    gate2 = mod_ref[0, 5:6, :]
    hff = (_rms(x) * (g2_ref[...] * (1.0 + scale2)) + shift2).astype(BF16)
    n_back = FFN_CONV_WIDTH - 1

    def conv_block(col):
        cs = slice(col, col + FB)
        hist = _history(jnp.dot(hff, wup_ref[:, cs], preferred_element_type=F32), tail_buf, cs, n_back)
        out = cb_ref[:, cs]
        for kk in range(FFN_CONV_WIDTH):
            out = out + cw_ref[kk:kk + 1, cs] * hist[kk * G:kk * G + T]
        return out

    for blk in range(D_FF // FB):
        g = conv_block(blk * FB)
        val = conv_block(D_FF + blk * FB)
        act_buf[:, blk * FB:(blk + 1) * FB] = (g * jax.nn.sigmoid(g) * val).astype(BF16)

    for piece in range(T // ROW_PIECE):
        rows = slice(piece * ROW_PIECE, (piece + 1) * ROW_PIECE)
        y = jnp.dot(act_buf[rows, :], wdn_ref[...], preferred_element_type=F32)
        x_p = jnp.concatenate([x_ref[0, k, rows, :] for k in range(N_SLAB)], axis=1)
        res = _rms(x_p + gate2 * y) * gf_ref[...]
        out_buf[slot, piece * ROW_PIECE // G:(piece + 1) * ROW_PIECE // G] = res.reshape(ROW_PIECE // G, G, D_MODEL)
    for cp in _strand_copies(o_hbm, out_buf, out_sem, tile, slot, n_seq, False):
        cp.start()

    @pl.when(tile == n_tiles - 1)
    def _():
        for cp in _strand_copies(o_hbm, out_buf, out_sem, tile - 1, 1 - slot, n_seq, False):
            cp.wait()
        for cp in _strand_copies(o_hbm, out_buf, out_sem, tile, slot, n_seq, False):
            cp.wait()


def _ffn(x1, mod, norm2_g, w_up, ffn_conv_w, ffn_conv_b, w_down, final_norm_g):
    bsz, n_slab, s, _ = x1.shape
    d = D_MODEL
    T = TIME_TILE

    def const(shape):
        return pl.BlockSpec(shape, lambda b, j: (0,) * len(shape), pipeline_mode=pl.Buffered(1))

    return pl.pallas_call(
        _ffn_kernel,
        grid=(bsz, s // T),
        in_specs=[
            pl.BlockSpec((1, n_slab, T, LANES), lambda b, j: (b, 0, j, 0)),
            pl.BlockSpec((1, N_MOD, d), lambda b, j: (b, 0, 0)),
            const((1, d)),
            const((d, 2 * D_FF)),
            const((FFN_CONV_WIDTH, 2 * D_FF)),
            const((1, 2 * D_FF)),
            const((D_FF, d)),
            const((1, d)),
        ],
        out_specs=pl.BlockSpec(memory_space=pl.ANY),
        out_shape=jax.ShapeDtypeStruct((bsz, s, d), F32),
        scratch_shapes=[
            pltpu.VMEM((T, D_FF), BF16),
            pltpu.VMEM((2, STRAND, SUBLANES, d), F32),
            pltpu.SemaphoreType.DMA((2,)),
            pltpu.VMEM(((FFN_CONV_WIDTH - 1) * SUBLANES, 2 * D_FF), F32),
        ],
        compiler_params=pltpu.CompilerParams(
            dimension_semantics=("arbitrary", "arbitrary"),
            vmem_limit_bytes=VMEM_LIMIT_BYTES),
        name="channel_mixer",
    )(x1, mod, norm2_g.reshape(1, d), w_up.astype(BF16), ffn_conv_w,
      ffn_conv_b.reshape(1, -1), w_down.astype(BF16), final_norm_g.reshape(1, d))


def kernel(x, c, w_ada, b_ada, norm1_g, w_in, qk_conv_w, qk_conv_b, b_igate, b_fgate,
           mlstm_norm_g, pool_w, pool_scale, w_out, norm2_g, w_up, ffn_conv_w, ffn_conv_b,
           w_down, final_norm_g):
    bsz = x.shape[0]
    mod = _ada_mod(c, w_ada[0], b_ada[0]).reshape(bsz, N_MOD, D_MODEL)
    x1 = _mixer(x, mod, norm1_g[0], w_in[0], qk_conv_w[0], qk_conv_b[0], b_igate[0], b_fgate[0],
                mlstm_norm_g[0], pool_w[0], pool_scale[0], w_out[0])
    return _ffn(x1, mod, norm2_g[0], w_up[0], ffn_conv_w[0], ffn_conv_b[0], w_down[0], final_norm_g)
```

```python
import jax
import jax.numpy as jnp
from jax import lax
from jax.experimental import pallas as pl
from jax.experimental.pallas import tpu as pltpu

D_MODEL = 1024
MLSTM_HEADS = 4
D_MLSTM = 512
HEAD_DIM = 128
QK_CONV_WIDTH = 4
POOL_WINDOWS = (2, 4, 8, 16)
D_POOL = 512
POOL_GROUP_DIM = 128
D_FF = 2816
FFN_CONV_WIDTH = 3
N_MOD = 6
EPS = 1e-6

SUBLANES = 8
LANES = 128
VMEM_LIMIT_BYTES = 56 * 1024 * 1024

TIME_TILE = 1024
STRAND = TIME_TILE // SUBLANES
CHUNK_STRANDS = 2
MLSTM_CHUNK = CHUNK_STRANDS * STRAND
FF_BLOCK = 256
SCAN_BLOCK = 8
OUT_ROWS = 512
ROW_PIECE = 256
N_SLAB = D_MODEL // LANES

F32 = jnp.float32
BF16 = jnp.bfloat16


def _rms(x):
    return x * lax.rsqrt(jnp.mean(x * x, axis=-1, keepdims=True) + EPS)


def _log_sigmoid(x):
    return jnp.minimum(x, 0.0) - jnp.log1p(jnp.exp(-jnp.abs(x)))


def _strand_rows(s):
    return pl.ds(s, STRAND, stride=SUBLANES)


def _strand_copies(hbm_ref, vmem_ref, sem, tile, slot, tiles_per_seq, to_vmem):
    b = tile // tiles_per_seq
    t0 = (tile % tiles_per_seq) * TIME_TILE
    copies = []
    for s in range(SUBLANES):
        hbm = hbm_ref.at[b, pl.ds(t0 + s * STRAND, STRAND), :]
        vmem = vmem_ref.at[slot, :, s, :]
        src, dst = (hbm, vmem) if to_vmem else (vmem, hbm)
        copies.append(pltpu.make_async_copy(src, dst, sem.at[slot]))
    return copies


def _tile_groups(group, n):
    return jnp.concatenate([group] * n, axis=0)


def _history(y, tail_ref, cols, n):
    T, W = y.shape
    G = SUBLANES
    last_sublane = lax.broadcasted_iota(jnp.int32, (G, W), 0) == G - 1
    groups = []
    for m in range(n, 0, -1):
        cur = y[T - G * m:T - G * (m - 1)]
        prev = tail_ref[(n - m) * G:(n - m + 1) * G, cols]
        groups.append(pltpu.roll(jnp.where(last_sublane, prev, cur), 1, 0))
    tail_ref[:, cols] = y[T - G * n:T]
    return jnp.concatenate(groups + [y], axis=0)


def _ada_kernel(c_ref, w_ref, b_ref, o_ref):
    cond = c_ref[...]
    cond = cond * jax.nn.sigmoid(cond)
    o_ref[...] = jnp.dot(cond.astype(BF16), w_ref[...].astype(BF16),
                         preferred_element_type=F32) + b_ref[...]


def _ada_mod(c, w_ada, b_ada):
    bsz = c.shape[0]
    n = w_ada.shape[1]
    blk = D_MODEL
    return pl.pallas_call(
        _ada_kernel,
        grid=(n // blk,),
        in_specs=[pl.BlockSpec((bsz, D_MODEL), lambda i: (0, 0)),
                  pl.BlockSpec((D_MODEL, blk), lambda i: (0, i)),
                  pl.BlockSpec((1, blk), lambda i: (0, i))],
        out_specs=pl.BlockSpec((bsz, blk), lambda i: (0, i)),
        out_shape=jax.ShapeDtypeStruct((bsz, n), F32),
        name="ada_mod",
    )(c, w_ada, b_ada.reshape(1, n))


SC_E1, SC_W_INTER, SC_E_NEGM, SC_W_STATE, SC_R = range(5)


def _mixer_kernel(x_hbm, mod_ref, g1_ref, win_ref, wg_ref, bg_ref, cw_ref, cb_ref, gm_ref,
                  pw_ref, ps_ref, wout_ref, o_ref, ssq_ref,
                  x_buf, x_sem, qkv_buf, sc_buf, h_buf, cat_buf, qk_tail, u_tail, c_state, m_state):
    T = TIME_TILE
    G = SUBLANES
    NG = T // G
    NC = T // MLSTM_CHUNK
    H = MLSTM_HEADS
    j = pl.program_id(1)
    n_seq = pl.num_programs(1)
    tile = pl.program_id(0) * n_seq + j
    n_tiles = pl.num_programs(0) * n_seq
    slot = tile % 2

    @pl.when(tile == 0)
    def _():
        for cp in _strand_copies(x_hbm, x_buf, x_sem, tile, slot, n_seq, True):
            cp.start()

    @pl.when(tile + 1 < n_tiles)
    def _():
        for cp in _strand_copies(x_hbm, x_buf, x_sem, tile + 1, 1 - slot, n_seq, True):
            cp.start()

    for cp in _strand_copies(x_hbm, x_buf, x_sem, tile, slot, n_seq, True):
        cp.wait()

    @pl.when(j == 0)
    def _():
        qk_tail[...] = jnp.zeros_like(qk_tail)
        u_tail[...] = jnp.zeros_like(u_tail)
        c_state[...] = jnp.zeros_like(c_state)
        m_state[...] = jnp.zeros_like(m_state)

    x = x_buf[slot].reshape(T, D_MODEL)

    shift1 = mod_ref[0, 0:1, :]
    scale1 = mod_ref[0, 1:2, :]
    gate1 = mod_ref[0, 2:3, :]
    hmix = (_rms(x) * (g1_ref[...] * (1.0 + scale1)) + shift1).astype(BF16)

    def proj(lo, hi):
        return jnp.dot(hmix, win_ref[:, lo:hi], preferred_element_type=F32)


    g_col = jnp.dot(hmix, wg_ref[...], preferred_element_type=F32) + bg_ref[...]
    u = proj(4 * D_MLSTM, 4 * D_MLSTM + D_POOL)
    i_col = g_col[:, 0:LANES]
    f_col = _log_sigmoid(g_col[:, LANES:])

    zero_g = jnp.zeros((G, LANES), F32)
    ninf_g = jnp.full((G, LANES), -jnp.inf, F32)
    rows_b = SCAN_BLOCK * G
    blocks = []
    for k in range(NG // SCAN_BLOCK):
        a = f_col[k * rows_b:(k + 1) * rows_b]
        b = i_col[k * rows_b:(k + 1) * rows_b]
        d = 1
        while d < SCAN_BLOCK:
            a_prev = jnp.concatenate([_tile_groups(zero_g, d), a[0:rows_b - G * d]], axis=0)
            b_prev = jnp.concatenate([_tile_groups(ninf_g, d), b[0:rows_b - G * d]], axis=0)
            b = jnp.maximum(b_prev + a, b)
            a = a_prev + a
            d *= 2
        blocks.append((a, b))
    tot_a, tot_b = zero_g, ninf_g
    before = []
    for a, b in blocks:
        before.append((tot_a, tot_b))
        tot_b = jnp.maximum(tot_b + a[rows_b - G:], b[rows_b - G:])
        tot_a = tot_a + a[rows_b - G:]
    sub = lax.broadcasted_iota(jnp.int32, (G, LANES), 0)
    left_a, left_b = zero_g, ninf_g
    lv = 1
    while lv < CHUNK_STRANDS:
        pos = sub % (2 * lv)
        a_carry, b_carry = zero_g, ninf_g
        for off in range(1, lv + 1):
            take = pos == lv - 1 + off
            a_carry = jnp.where(take, pltpu.roll(tot_a, off, 0), a_carry)
            b_carry = jnp.where(take, pltpu.roll(tot_b, off, 0), b_carry)
        left_b = jnp.maximum(b_carry + left_a, left_b)
        left_a = a_carry + left_a
        tot_b = jnp.maximum(b_carry + tot_a, tot_b)
        tot_a = a_carry + tot_a
        lv *= 2
    a_parts, b_parts = [], []
    for (a, b), (pre_a, pre_b) in zip(blocks, before):
        pre_b = jnp.maximum(left_b + pre_a, pre_b)
        pre_a = left_a + pre_a
        b_parts.append(jnp.maximum(_tile_groups(pre_b, SCAN_BLOCK) + a, b))
        a_parts.append(a + _tile_groups(pre_a, SCAN_BLOCK))
    b_col = jnp.concatenate(a_parts, axis=0)
    a_col = jnp.concatenate(b_parts, axis=0)
    o_pre = proj(3 * D_MLSTM, 4 * D_MLSTM)

    row = lax.broadcasted_iota(jnp.int32, (T, 1), 0)
    t_glob = j * T + (row % G) * STRAND + row // G
    n_tail = max(POOL_WINDOWS) - 1
    for g, win in enumerate(POOL_WINDOWS):
        gs = slice(g * POOL_GROUP_DIM, (g + 1) * POOL_GROUP_DIM)
        u_g = u[:, gs]
        acc = _history(u_g, u_tail.at[(n_tail - (win - 1)) * G:n_tail * G, :], gs, win - 1)
        d = 1
        while d < win:
            acc = acc[G * d:] + acc[:acc.shape[0] - G * d]
            d *= 2
        count = jnp.minimum(t_glob + 1, win).astype(F32)
        pooled = acc / count - u_g
        h_p = jnp.dot(pooled.astype(BF16), pw_ref[g], preferred_element_type=F32) * ps_ref[:, gs]
        cat_buf[:, D_MLSTM + g * POOL_GROUP_DIM:D_MLSTM + (g + 1) * POOL_GROUP_DIM] = h_p.astype(BF16)
    o_gate = jax.nn.sigmoid(o_pre)

    qk_pre = proj(0, 2 * D_MLSTM)

    b_end = b_col[T - G:T]
    a_end = a_col[T - G:T]
    m_c = m_state[0:1, :]
    m_prev = jnp.zeros((G, LANES), F32)
    m_new = jnp.zeros((G, LANES), F32)
    b_last = jnp.zeros((G, LANES), F32)
    decays = []
    for c in range(NC):
        end_row = CHUNK_STRANDS * (c + 1) - 1
        b_sum = b_end[end_row:end_row + 1, :]
        m_n = jnp.maximum(m_c + b_sum, a_end[end_row:end_row + 1, :])
        decays.append(jnp.exp(b_sum + m_c - m_n))
        in_chunk = (sub // CHUNK_STRANDS) == c
        m_prev = jnp.where(in_chunk, m_c, m_prev)
        m_new = jnp.where(in_chunk, m_n, m_new)
        b_last = jnp.where(in_chunk, b_sum, b_last)
        m_c = m_n
    m_state[0:1, :] = m_c
    m_prev = _tile_groups(m_prev, NG)
    m_new = _tile_groups(m_new, NG)
    b_last = _tile_groups(b_last, NG)
    m_t = jnp.maximum(b_col + m_prev, a_col)
    sc_buf[SC_E1] = b_col - m_t
    sc_buf[SC_W_INTER] = jnp.exp(b_col + m_prev - m_t)
    sc_buf[SC_E_NEGM] = jnp.exp(-m_t)
    sc_buf[SC_W_STATE] = jnp.exp(b_last - b_col + i_col - m_new)
    sc_buf[SC_R] = i_col - b_col

    v_all = proj(2 * D_MLSTM, 3 * D_MLSTM)

    n_back = QK_CONV_WIDTH - 1
    qk_hist = _history(qk_pre, qk_tail, slice(None), n_back)
    conv = cb_ref[...]
    for kk in range(QK_CONV_WIDTH):
        conv = conv + cw_ref[kk:kk + 1, :] * qk_hist[kk * G:kk * G + T]
    qk = conv * jax.nn.sigmoid(conv)
    for h in range(H):
        hs = slice(h * HEAD_DIM, (h + 1) * HEAD_DIM)
        qkv_buf[h] = qk[:, hs]
        qkv_buf[H + h] = qk[:, D_MLSTM + h * HEAD_DIM:D_MLSTM + (h + 1) * HEAD_DIM] * (HEAD_DIM ** -0.5)
        qkv_buf[2 * H + h] = v_all[:, hs]

    t_idx = lax.broadcasted_iota(jnp.int32, (MLSTM_CHUNK, MLSTM_CHUNK), 0)
    s_idx = lax.broadcasted_iota(jnp.int32, (MLSTM_CHUNK, MLSTM_CHUNK), 1)
    causal = s_idx <= t_idx
    ones = jnp.ones((MLSTM_CHUNK, HEAD_DIM), F32)

    states = [c_state[h] for h in range(H)]
    for c in range(NC):
        rows = [_strand_rows(CHUNK_STRANDS * c + k) for k in range(CHUNK_STRANDS)]

        def gather(ref, k):
            return jnp.concatenate([ref[k, r, :] for r in rows], axis=0)

        e1 = gather(sc_buf, SC_E1)
        w_inter = gather(sc_buf, SC_W_INTER)
        e_negm = gather(sc_buf, SC_E_NEGM)
        w_state = gather(sc_buf, SC_W_STATE)
        r_t = gather(sc_buf, SC_R).T
        for h in range(H):
            q_c = gather(qkv_buf, h).astype(BF16)
            k_c = gather(qkv_buf, H + h).astype(BF16)
            v_aug = jnp.concatenate([gather(qkv_buf, 2 * H + h), ones], axis=1)
            p = e1[:, h:h + 1] + r_t[h:h + 1, :]
            dm = jnp.exp(jnp.where(causal, p, -jnp.inf))
            scores = lax.dot_general(q_c, k_c, (((1,), (1,)), ((), ())),
                                     preferred_element_type=F32) * dm
            res = (jnp.dot(scores.astype(BF16), v_aug.astype(BF16), preferred_element_type=F32)
                   + w_inter[:, h:h + 1] * jnp.dot(q_c, states[h].astype(BF16),
                                                   preferred_element_type=F32))
            den = jnp.maximum(jnp.abs(res[:, HEAD_DIM:]), e_negm[:, h:h + 1])
            h_c = res[:, 0:HEAD_DIM] / den
            for k, r in enumerate(rows):
                h_buf[h, r, :] = h_c[k * STRAND:(k + 1) * STRAND]
            states[h] = decays[c][:, h:h + 1] * states[h] + lax.dot_general(
                k_c, (w_state[:, h:h + 1] * v_aug).astype(BF16), (((0,), (0,)), ((), ())),
                preferred_element_type=F32)
    for h in range(H):
        c_state[h] = states[h]

    h_all = jnp.concatenate([h_buf[h] for h in range(H)], axis=1)
    h_m = _rms(h_all) * gm_ref[...] * o_gate
    cat_buf[:, 0:D_MLSTM] = h_m.astype(BF16)

    for piece in range(T // OUT_ROWS):
        rows = slice(piece * OUT_ROWS, (piece + 1) * OUT_ROWS)
        mixed = jnp.dot(cat_buf[rows, :], wout_ref[...], preferred_element_type=F32)
        x1 = x_buf[slot, piece * OUT_ROWS // G:(piece + 1) * OUT_ROWS // G].reshape(OUT_ROWS, D_MODEL) + gate1 * mixed
        for slab in range(N_SLAB):
            o_ref[0, slab, rows, :] = x1[:, slab * LANES:(slab + 1) * LANES]
        ssq_ref[0, rows, :] = jnp.sum(x1 * x1, axis=-1, keepdims=True)


def _mixer(x, mod, norm1_g, w_in, qk_conv_w, qk_conv_b, b_igate, b_fgate, mlstm_norm_g,
           pool_w, pool_scale, w_out):
    bsz, s, d = x.shape
    T = TIME_TILE
    H = MLSTM_HEADS
    n_main = 4 * D_MLSTM + D_POOL
    w_main = w_in[:, :n_main].astype(BF16)
    w_i = w_in[:, n_main:n_main + H]
    w_f = w_in[:, n_main + H:]
    wg = jnp.concatenate([jnp.pad(w_i, ((0, 0), (0, LANES - H))),
                          jnp.pad(w_f, ((0, 0), (0, LANES - H)))], axis=1).astype(BF16)
    bg = jnp.concatenate([jnp.pad(b_igate, (0, LANES - H)),
                          jnp.pad(b_fgate, (0, LANES - H))]).reshape(1, 2 * LANES)

    def const(shape):
        return pl.BlockSpec(shape, lambda b, j: (0,) * len(shape), pipeline_mode=pl.Buffered(1))

    return pl.pallas_call(
        _mixer_kernel,
        grid=(bsz, s // T),
        in_specs=[
            pl.BlockSpec(memory_space=pl.ANY),
            pl.BlockSpec((1, N_MOD, d), lambda b, j: (b, 0, 0)),
            const((1, d)),
            const((d, n_main)),
            const((d, 2 * LANES)),
            const((1, 2 * LANES)),
            const((QK_CONV_WIDTH, 2 * D_MLSTM)),
            const((1, 2 * D_MLSTM)),
            const((1, D_MLSTM)),
            const((len(POOL_WINDOWS), POOL_GROUP_DIM, POOL_GROUP_DIM)),
            const((1, D_POOL)),
            const((D_MLSTM + D_POOL, d)),
        ],
        out_specs=[pl.BlockSpec((1, N_SLAB, T, LANES), lambda b, j: (b, 0, j, 0)),
                   pl.BlockSpec((1, T, 1), lambda b, j: (b, j, 0))],
        out_shape=[jax.ShapeDtypeStruct((bsz, N_SLAB, s, LANES), F32),
                   jax.ShapeDtypeStruct((bsz, s, 1), F32)],
        scratch_shapes=[
            pltpu.VMEM((2, STRAND, SUBLANES, d), F32),
            pltpu.SemaphoreType.DMA((2,)),
            pltpu.VMEM((3 * MLSTM_HEADS, T, LANES), F32),
            pltpu.VMEM((5, T, LANES), F32),
            pltpu.VMEM((MLSTM_HEADS, T, LANES), F32),
            pltpu.VMEM((T, D_MLSTM + D_POOL), BF16),
            pltpu.VMEM(((QK_CONV_WIDTH - 1) * SUBLANES, 2 * D_MLSTM), F32),
            pltpu.VMEM(((max(POOL_WINDOWS) - 1) * SUBLANES, D_POOL), F32),
            pltpu.VMEM((MLSTM_HEADS, HEAD_DIM, 2 * HEAD_DIM), F32),
            pltpu.VMEM((SUBLANES, LANES), F32),
        ],
        compiler_params=pltpu.CompilerParams(
            dimension_semantics=("arbitrary", "arbitrary"),
            vmem_limit_bytes=VMEM_LIMIT_BYTES),
        name="token_mixer",
    )(x, mod, norm1_g.reshape(1, d), w_main, wg, bg, qk_conv_w,
      qk_conv_b.reshape(1, -1), mlstm_norm_g.reshape(1, -1), pool_w.astype(BF16),
      pool_scale.reshape(1, -1), w_out.astype(BF16))


def _ffn_kernel(x_ref, ssq_ref, mod_ref, g2_ref, wup_ref, cw_ref, cb_ref, wdn_ref, gf_ref, o_hbm,
                act_buf, out_buf, out_sem, tail_buf):
    T = TIME_TILE
    FB = FF_BLOCK
    G = SUBLANES
    j = pl.program_id(1)
    n_seq = pl.num_programs(1)
    tile = pl.program_id(0) * n_seq + j
    n_tiles = pl.num_programs(0) * n_seq
    slot = tile % 2

    @pl.when(tile >= 2)
    def _():
        for cp in _strand_copies(o_hbm, out_buf, out_sem, tile - 2, slot, n_seq, False):
            cp.wait()

    @pl.when(j == 0)
    def _():
        tail_buf[...] = jnp.zeros_like(tail_buf)

    x = jnp.concatenate([x_ref[0, k] for k in range(N_SLAB)], axis=1)
    shift2 = mod_ref[0, 3:4, :]
    scale2 = mod_ref[0, 4:5, :]
    gate2 = mod_ref[0, 5:6, :]
    inv_rms = lax.rsqrt(ssq_ref[0] * (1.0 / D_MODEL) + EPS)
    hff = (x * inv_rms * (g2_ref[...] * (1.0 + scale2)) + shift2).astype(BF16)
    n_back = FFN_CONV_WIDTH - 1

    def conv_block(col):
        cs = slice(col, col + FB)
        hist = _history(jnp.dot(hff, wup_ref[:, cs], preferred_element_type=F32), tail_buf, cs, n_back)
        out = cb_ref[:, cs]
        for kk in range(FFN_CONV_WIDTH):
            out = out + cw_ref[kk:kk + 1, cs] * hist[kk * G:kk * G + T]
        return out

    for blk in range(D_FF // FB):
        g = conv_block(blk * FB)
        val = conv_block(D_FF + blk * FB)
        act_buf[:, blk * FB:(blk + 1) * FB] = (g * jax.nn.sigmoid(g) * val).astype(BF16)

    for piece in range(T // ROW_PIECE):
        rows = slice(piece * ROW_PIECE, (piece + 1) * ROW_PIECE)
        y = jnp.dot(act_buf[rows, :], wdn_ref[...], preferred_element_type=F32)
        x_p = jnp.concatenate([x_ref[0, k, rows, :] for k in range(N_SLAB)], axis=1)
        res = _rms(x_p + gate2 * y) * gf_ref[...]
        out_buf[slot, piece * ROW_PIECE // G:(piece + 1) * ROW_PIECE // G] = res.reshape(ROW_PIECE // G, G, D_MODEL)
    for cp in _strand_copies(o_hbm, out_buf, out_sem, tile, slot, n_seq, False):
        cp.start()

    @pl.when(tile == n_tiles - 1)
    def _():
        for cp in _strand_copies(o_hbm, out_buf, out_sem, tile - 1, 1 - slot, n_seq, False):
            cp.wait()
        for cp in _strand_copies(o_hbm, out_buf, out_sem, tile, slot, n_seq, False):
            cp.wait()


def _ffn(x1, ssq, mod, norm2_g, w_up, ffn_conv_w, ffn_conv_b, w_down, final_norm_g):
    bsz, n_slab, s, _ = x1.shape
    d = D_MODEL
    T = TIME_TILE

    def const(shape):
        return pl.BlockSpec(shape, lambda b, j: (0,) * len(shape), pipeline_mode=pl.Buffered(1))

    return pl.pallas_call(
        _ffn_kernel,
        grid=(bsz, s // T),
        in_specs=[
            pl.BlockSpec((1, n_slab, T, LANES), lambda b, j: (b, 0, j, 0)),
            pl.BlockSpec((1, T, 1), lambda b, j: (b, j, 0)),
            pl.BlockSpec((1, N_MOD, d), lambda b, j: (b, 0, 0)),
            const((1, d)),
            const((d, 2 * D_FF)),
            const((FFN_CONV_WIDTH, 2 * D_FF)),
            const((1, 2 * D_FF)),
            const((D_FF, d)),
            const((1, d)),
        ],
        out_specs=pl.BlockSpec(memory_space=pl.ANY),
        out_shape=jax.ShapeDtypeStruct((bsz, s, d), F32),
        scratch_shapes=[
            pltpu.VMEM((T, D_FF), BF16),
            pltpu.VMEM((2, STRAND, SUBLANES, d), F32),
            pltpu.SemaphoreType.DMA((2,)),
            pltpu.VMEM(((FFN_CONV_WIDTH - 1) * SUBLANES, 2 * D_FF), F32),
        ],
        compiler_params=pltpu.CompilerParams(
            dimension_semantics=("arbitrary", "arbitrary"),
            vmem_limit_bytes=VMEM_LIMIT_BYTES),
        name="channel_mixer",
    )(x1, ssq, mod, norm2_g.reshape(1, d), w_up.astype(BF16), ffn_conv_w,
      ffn_conv_b.reshape(1, -1), w_down.astype(BF16), final_norm_g.reshape(1, d))


def kernel(x, c, w_ada, b_ada, norm1_g, w_in, qk_conv_w, qk_conv_b, b_igate, b_fgate,
           mlstm_norm_g, pool_w, pool_scale, w_out, norm2_g, w_up, ffn_conv_w, ffn_conv_b,
           w_down, final_norm_g):
    bsz = x.shape[0]
    mod = _ada_mod(c, w_ada[0], b_ada[0]).reshape(bsz, N_MOD, D_MODEL)
    x1, ssq = _mixer(x, mod, norm1_g[0], w_in[0], qk_conv_w[0], qk_conv_b[0], b_igate[0], b_fgate[0],
                     mlstm_norm_g[0], pool_w[0], pool_scale[0], w_out[0])
    return _ffn(x1, ssq, mod, norm2_g[0], w_up[0], ffn_conv_w[0], ffn_conv_b[0], w_down[0], final_norm_g)
```
